```python
import jax, jax.numpy as jnp
from jax import lax
import numpy as np

D_MODEL = 4096
BATCH = 4
SEQ = 2048
DEPTH = 1
DEC_BATCH = 128
DEC_SEQ = 4
PAST_LEN = 16384
PAGE_SIZE = 128

D_MIX = D_MODEL
D_POOL = D_MIX // 2
D_GATE = D_MIX - D_POOL
POOL_WINDOWS = (2, 4, 8, 16)
N_POOL_GROUPS = len(POOL_WINDOWS)
POOL_GC = D_POOL // N_POOL_GROUPS
POOL_STATE = max(POOL_WINDOWS) - 1
N_GATE_HEADS = 4
GATE_HD = D_GATE // N_GATE_HEADS
CHUNK = 128
N_MEM = 256
N_XHEADS = 4
X_HD = D_MODEL // N_XHEADS
D_FF = ((-(-8 * D_MODEL // 3) + 255) // 256) * 256
EPS = 1e-6

kernel_name = "hybrid_pool_gmlp_memxattn_decoder_step"


def _rmsnorm(x, g):
    xf = x.astype(jnp.float32)
    xf = xf * lax.rsqrt(jnp.mean(xf * xf, axis=-1, keepdims=True) + EPS)
    return (xf * g.astype(jnp.float32)).astype(x.dtype)


def _pool_mixer(p, prefix, pos0, w_pool, pool_scale):
    B, L, C = p.shape
    P = prefix.shape[1]
    ext = jnp.concatenate([prefix, p], axis=1)
    extf = ext.astype(jnp.float32)
    cs = jnp.pad(jnp.cumsum(extf, axis=1), ((0, 0), (1, 0), (0, 0)))
    end = cs[:, P + 1:P + 1 + L]
    pos = pos0 + jnp.arange(L)
    means = []
    for g, w in enumerate(POOL_WINDOWS):
        sl = slice(g * POOL_GC, (g + 1) * POOL_GC)
        start = cs[:, P + 1 - w:P + 1 - w + L, sl]
        cnt = jnp.minimum(pos + 1, w).astype(jnp.float32)[None, :, None]
        means.append((end[..., sl] - start) / cnt)
    pooled = (jnp.concatenate(means, axis=-1) - extf[:, P:]).astype(p.dtype)
    mixed = jnp.einsum('blgc,gcd->blgd', pooled.reshape(B, L, N_POOL_GROUPS, POOL_GC), w_pool)
    out = mixed.reshape(B, L, C) * pool_scale
    new_state = ext[:, -POOL_STATE:]
    return out, new_state


def _spatial_gate(u, v, w_s, b_s):
    B, L, H, hd = v.shape
    c = min(L, CHUNK)
    n = -(-L // c)
    pad = n * c - L
    vc = jnp.pad(v, ((0, 0), (0, pad), (0, 0), (0, 0))).reshape(B, n, c, H, hd)
    mask = jnp.tril(jnp.ones((c, c), dtype=bool))
    w = jnp.where(mask[None], w_s[:, :c, :c], 0).astype(v.dtype)
    mixed = jnp.einsum('hts,bnshd->bnthd', w, vc) + b_s[:, :c].T[None, None, :, :, None]
    mixed = mixed.reshape(B, n * c, H, hd)[:, :L]
    return u * mixed


def _mem_kv(mem, g_mem, w_k, w_v):
    B, M, _ = mem.shape
    hm = _rmsnorm(mem, g_mem)
    k = (hm @ w_k).reshape(B, M, N_XHEADS, X_HD)
    v = (hm @ w_v).reshape(B, M, N_XHEADS, X_HD)
    return k, v


def _layer(x, mem_k, mem_v, pool_prefix, pos0, g_mix, w_in, g_v, w_pool, pool_scale, w_s, b_s, w_out,
           g_xattn, w_q, w_o, g_ffn, w_gate_up, w_down):
    B, L, _ = x.shape
    h = _rmsnorm(x, g_mix)
    proj = h @ w_in
    p = proj[..., :D_POOL]
    z = jax.nn.gelu(proj[..., D_POOL:], approximate=False)
    u = z[..., :D_GATE].reshape(B, L, N_GATE_HEADS, GATE_HD)
    v = _rmsnorm(z[..., D_GATE:].reshape(B, L, N_GATE_HEADS, GATE_HD), g_v.reshape(N_GATE_HEADS, GATE_HD))
    pool_out, new_pool = _pool_mixer(p, pool_prefix, pos0, w_pool, pool_scale)
    gate_out = _spatial_gate(u, v, w_s, b_s).reshape(B, L, D_GATE)
    x = x + jnp.concatenate([pool_out, gate_out], axis=-1) @ w_out
    hq = _rmsnorm(x, g_xattn)
    q = (hq @ w_q).reshape(B, L, N_XHEADS, X_HD)
    s = jnp.einsum('blhd,bmhd->bhlm', q, mem_k).astype(jnp.float32) * (X_HD ** -0.5)
    a = jax.nn.softmax(s, axis=-1).astype(mem_v.dtype)
    o = jnp.einsum('bhlm,bmhd->blhd', a, mem_v).reshape(B, L, D_MODEL)
    x = x + o @ w_o
    hf = _rmsnorm(x, g_ffn)
    gu = hf @ w_gate_up
    x = x + (jax.nn.silu(gu[..., :D_FF]) * gu[..., D_FF:]) @ w_down
    return x, new_pool, v.reshape(B, L, D_GATE)


def setup_inputs(seed: int = 0) -> dict:
    key = jax.random.key(seed)
    ks = jax.random.split(key, 32)
    f32 = jnp.float32

    def nrm(k, shape, scale):
        return jax.random.normal(k, shape, f32) * scale

    def gain(k, shape):
        return 1.0 + 0.02 * jax.random.normal(k, shape, f32)

    D = D_MODEL
    return {
        "x_prompt": nrm(ks[0], (BATCH, SEQ, D), 1.0),
        "x_sample": nrm(ks[1], (DEC_BATCH, DEC_SEQ, D), 1.0),
        "cache_mem_k": nrm(ks[2], (DEPTH, DEC_BATCH, N_MEM, N_XHEADS, X_HD), 1.0),
        "cache_mem_v": nrm(ks[3], (DEPTH, DEC_BATCH, N_MEM, N_XHEADS, X_HD), 1.0),
        "state_pool": nrm(ks[4], (DEPTH, DEC_BATCH, POOL_STATE, D_POOL), 1.0),
        "mem_prompt": nrm(ks[5], (BATCH, N_MEM, D), 1.0),
        "g_mix": gain(ks[6], (DEPTH, D)),
        "w_in": nrm(ks[7], (DEPTH, D, D_POOL + 2 * D_GATE), D ** -0.5),
        "g_v": gain(ks[8], (DEPTH, D_GATE)),
        "w_pool": nrm(ks[9], (DEPTH, N_POOL_GROUPS, POOL_GC, POOL_GC), POOL_GC ** -0.5),
        "pool_scale": gain(ks[10], (DEPTH, D_POOL)),
        "w_s": nrm(ks[11], (DEPTH, N_GATE_HEADS, CHUNK, CHUNK), CHUNK ** -0.5),
        "b_s": 1.0 + 0.1 * jax.random.normal(ks[12], (DEPTH, N_GATE_HEADS, CHUNK), f32),
        "w_out": nrm(ks[13], (DEPTH, D_MIX, D), D_MIX ** -0.5),
        "g_xattn": gain(ks[14], (DEPTH, D)),
        "g_mem": gain(ks[15], (DEPTH, D)),
        "w_q": nrm(ks[16], (DEPTH, D, D), D ** -0.5),
        "w_k": nrm(ks[17], (DEPTH, D, D), D ** -0.5),
        "w_v": nrm(ks[18], (DEPTH, D, D), D ** -0.5),
        "w_o": nrm(ks[19], (DEPTH, D, D), D ** -0.5),
        "g_ffn": gain(ks[20], (DEPTH, D)),
        "w_gate_up": nrm(ks[21], (DEPTH, D, 2 * D_FF), D ** -0.5),
        "w_down": nrm(ks[22], (DEPTH, D_FF, D), D_FF ** -0.5),
        "g_final": gain(ks[23], (D,)),
    }


def reference(x_prompt, x_sample, cache_mem_k, cache_mem_v, state_pool, mem_prompt, g_mix, w_in, g_v,
              w_pool, pool_scale, w_s, b_s, w_out, g_xattn, g_mem, w_q, w_k, w_v, w_o, g_ffn, w_gate_up,
              w_down, g_final):
    xp, xs = x_prompt, x_sample
    mk_list, mv_list, pool_p_list, pool_s_list, vs_list = [], [], [], [], []
    zero_prefix = jnp.zeros((xp.shape[0], POOL_STATE, D_POOL), xp.dtype)
    for l in range(DEPTH):
        lw = (g_mix[l], w_in[l], g_v[l], w_pool[l], pool_scale[l], w_s[l], b_s[l], w_out[l],
              g_xattn[l], w_q[l], w_o[l], g_ffn[l], w_gate_up[l], w_down[l])
        mk_p, mv_p = _mem_kv(mem_prompt, g_mem[l], w_k[l], w_v[l])
        xp, pool_p, _ = _layer(xp, mk_p, mv_p, zero_prefix, 0, *lw)
        xs, pool_s, v_s = _layer(xs, cache_mem_k[l], cache_mem_v[l], state_pool[l], PAST_LEN, *lw)
        mk_list.append(mk_p)
        mv_list.append(mv_p)
        pool_p_list.append(pool_p)
        pool_s_list.append(pool_s)
        vs_list.append(v_s)
    y_prompt = _rmsnorm(xp, g_final)
    y_sample = _rmsnorm(xs, g_final)
    return (y_prompt, y_sample, jnp.stack(mk_list), jnp.stack(mv_list), jnp.stack(pool_p_list),
            jnp.stack(pool_s_list), jnp.stack(vs_list))
```

```python
import functools
import math

import jax
import jax.numpy as jnp
from jax import lax
from jax.experimental import pallas as pl
from jax.experimental.pallas import tpu as pltpu

EPS = 1e-6
POOL_WINDOWS = (2, 4, 8, 16)
N_GROUPS = 4
CHUNK = 128
HALO = 16
PAST_LEN = 16384

V7X_VMEM_LIMIT_BYTES = 60000 * 1024

BF16 = jnp.bfloat16
F32 = jnp.float32


def _params(*sem):
    return pltpu.CompilerParams(dimension_semantics=sem, vmem_limit_bytes=V7X_VMEM_LIMIT_BYTES)


def _bdot(a, b):
    return jnp.dot(a, b, preferred_element_type=F32)


def _norm_rows_into(x_ref, g_ref, h_ref, rows=64):
    def body(c, carry):
        r0 = pl.multiple_of(c * rows, rows)
        x = x_ref[pl.ds(r0, rows), :]
        ms = jnp.mean(x * x, axis=-1, keepdims=True)
        h_ref[pl.ds(r0, rows), :] = (x * lax.rsqrt(ms + EPS) * g_ref[...]).astype(h_ref.dtype)
        return carry
    lax.fori_loop(0, x_ref.shape[0] // rows, body, 0)


def _gelu(x):
    return 0.5 * x * (1.0 + lax.erf(x * math.sqrt(0.5)))


def _norm_mm_kernel(x_ref, g_ref, w_ref, *refs, n_out, gelu_from):
    outs, h_ref = refs[:n_out], refs[n_out]
    j = pl.program_id(1)

    @pl.when(j == 0)
    def _():
        _norm_rows_into(x_ref, g_ref, h_ref)

    acc = _bdot(h_ref[...], w_ref[...].astype(BF16))
    if gelu_from is None:
        for o in outs:
            o[...] = acc.astype(o.dtype)
    else:
        @pl.when(j < gelu_from)
        def _():
            outs[0][...] = acc

        @pl.when(j >= gelu_from)
        def _():
            outs[0][...] = _gelu(acc)


def _norm_mm(x, g, w, *, tm, tn, out_dtypes, gelu_from_col=None, name):
    m, k = x.shape
    n = w.shape[1]
    kern = functools.partial(_norm_mm_kernel, n_out=len(out_dtypes),
                             gelu_from=None if gelu_from_col is None else gelu_from_col // tn)
    return pl.pallas_call(
        kern,
        grid=(m // tm, n // tn),
        in_specs=[
            pl.BlockSpec((tm, k), lambda i, j: (i, 0), pipeline_mode=pl.Buffered(1)),
            pl.BlockSpec((1, k), lambda i, j: (0, 0)),
            pl.BlockSpec((k, tn), lambda i, j: (0, j)),
        ],
        out_specs=[pl.BlockSpec((tm, tn), lambda i, j: (i, j)) for _ in out_dtypes],
        out_shape=[jax.ShapeDtypeStruct((m, n), d) for d in out_dtypes],
        scratch_shapes=[pltpu.VMEM((tm, k), BF16)],
        compiler_params=_params("arbitrary", "arbitrary"),
        name=name,
    )(x, g.reshape(1, k), w)


def _swiglu_kernel(x_ref, g_ref, wg_ref, wu_ref, o_ref, h_ref):
    @pl.when(pl.program_id(1) == 0)
    def _():
        _norm_rows_into(x_ref, g_ref, h_ref)

    h = h_ref[...]
    gate = _bdot(h, wg_ref[...].astype(BF16))
    up = _bdot(h, wu_ref[...].astype(BF16))
    o_ref[...] = (gate * (1.0 / (1.0 + jnp.exp(-gate))) * up).astype(o_ref.dtype)


def _swiglu(x, g, w_gate_up, *, tm, tn, name):
    m, k = x.shape
    d_ff = w_gate_up.shape[1] // 2
    nt = d_ff // tn
    return pl.pallas_call(
        _swiglu_kernel,
        grid=(m // tm, nt),
        in_specs=[
            pl.BlockSpec((tm, k), lambda i, j: (i, 0), pipeline_mode=pl.Buffered(1)),
            pl.BlockSpec((1, k), lambda i, j: (0, 0)),
            pl.BlockSpec((k, tn), lambda i, j: (0, j)),
            pl.BlockSpec((k, tn), lambda i, j: (0, j + nt)),
        ],
        out_specs=pl.BlockSpec((tm, tn), lambda i, j: (i, j)),
        out_shape=jax.ShapeDtypeStruct((m, d_ff), BF16),
        scratch_shapes=[pltpu.VMEM((tm, k), BF16)],
        compiler_params=_params("arbitrary", "arbitrary"),
        name=name,
    )(x, g.reshape(1, k), w_gate_up, w_gate_up)


def _mm_res_kernel(a_ref, w_ref, r_ref, o_ref):
    o_ref[...] = r_ref[...] + _bdot(a_ref[...], w_ref[...].astype(BF16))


def _mm_res(a, w, res, *, tm, tn, name):
    m, k = a.shape
    n = w.shape[1]
    return pl.pallas_call(
        _mm_res_kernel,
        grid=(m // tm, n // tn),
        in_specs=[
            pl.BlockSpec((tm, k), lambda i, j: (i, 0), pipeline_mode=pl.Buffered(1)),
            pl.BlockSpec((k, tn), lambda i, j: (0, j)),
            pl.BlockSpec((tm, tn), lambda i, j: (i, j)),
        ],
        out_specs=pl.BlockSpec((tm, tn), lambda i, j: (i, j)),
        out_shape=jax.ShapeDtypeStruct((m, n), F32),
        compiler_params=_params("arbitrary", "arbitrary"),
        name=name,
    )(a, w, res)


def _rmsnorm_kernel(x_ref, g_ref, o_ref):
    x = x_ref[...]
    ms = jnp.mean(x * x, axis=-1, keepdims=True)
    o_ref[...] = x * lax.rsqrt(ms + EPS) * g_ref[...]


def _rmsnorm(x, g, *, tm, name):
    m, k = x.shape
    return pl.pallas_call(
        _rmsnorm_kernel,
        grid=(m // tm,),
        in_specs=[pl.BlockSpec((tm, k), lambda i: (i, 0)), pl.BlockSpec((1, k), lambda i: (0, 0))],
        out_specs=pl.BlockSpec((tm, k), lambda i: (i, 0)),
        out_shape=jax.ShapeDtypeStruct((m, k), F32),
        compiler_params=_params("arbitrary"),
        name=name,
    )(x, g.reshape(1, k))


def _mix_prompt_kernel(p_ref, halo_ref, u_ref, v_ref, wp_ref, ps_ref, gv_ref, ws_ref, bs_ref,
                       o_ref, ext_ref, pooled_ref, *, rows):
    r = pl.program_id(1)
    d_pool = p_ref.shape[1]
    gc = d_pool // N_GROUPS

    @pl.when(r == 0)
    def _():
        ext_ref[0:HALO, :] = jnp.zeros((HALO, d_pool), F32)

    @pl.when(r > 0)
    def _():
        ext_ref[0:HALO, :] = halo_ref[...]

    ext_ref[HALO:HALO + rows, :] = p_ref[...]

    pos = r * rows + lax.broadcasted_iota(jnp.int32, (CHUNK, 1), 0)
    for g, w in enumerate(POOL_WINDOWS):
        cols = slice(g * gc, (g + 1) * gc)
        for c in range(rows // CHUNK):
            base = HALO + c * CHUNK
            tok = ext_ref[base:base + CHUNK, cols]
            acc = tok
            for back in range(1, w):
                acc = acc + ext_ref[base - back:base - back + CHUNK, cols]
            cnt = jnp.minimum(pos + (c * CHUNK + 1), w).astype(F32)
            pooled_ref[c * CHUNK:(c + 1) * CHUNK, :] = (acc / cnt - tok).astype(BF16)
        mixed = _bdot(pooled_ref[...], wp_ref[g].astype(BF16)) * ps_ref[:, cols]
        o_ref[:, cols] = mixed.astype(o_ref.dtype)

    row_id = lax.broadcasted_iota(jnp.int32, (CHUNK, CHUNK), 0)
    col_id = lax.broadcasted_iota(jnp.int32, (CHUNK, CHUNK), 1)
    for h in range(N_GROUPS):
        cols = slice(h * gc, (h + 1) * gc)
        w_tri = jnp.where(row_id >= col_id, ws_ref[h], 0.0).astype(BF16)
        bias = bs_ref[h]
        for c in range(rows // CHUNK):
            rs = slice(c * CHUNK, (c + 1) * CHUNK)
            vr = v_ref[rs, cols]
            ms = jnp.mean(vr * vr, axis=-1, keepdims=True)
            vn = (vr * lax.rsqrt(ms + EPS) * gv_ref[:, cols]).astype(BF16)
            mixed = _bdot(w_tri, vn) + bias
            o_ref[rs, d_pool + h * gc:d_pool + (h + 1) * gc] = (u_ref[rs, cols] * mixed).astype(o_ref.dtype)


def _mix_prompt(pz, w_pool, pool_scale, g_v, w_s, b_s, *, batch, seq, rows):
    d_pool = pool_scale.shape[-1]
    gc = d_pool // N_GROUPS
    blocks_per_seq = seq // rows

    def row_blk(b, r):
        return b * blocks_per_seq + r

    def halo_blk(b, r):
        return jnp.maximum((b * seq + r * rows) // HALO - 1, 0)

    kern = functools.partial(_mix_prompt_kernel, rows=rows)
    return pl.pallas_call(
        kern,
        grid=(batch, blocks_per_seq),
        in_specs=[
            pl.BlockSpec((rows, d_pool), lambda b, r: (row_blk(b, r), 0)),
            pl.BlockSpec((HALO, d_pool), lambda b, r: (halo_blk(b, r), 0)),
            pl.BlockSpec((rows, d_pool), lambda b, r: (row_blk(b, r), 1)),
            pl.BlockSpec((rows, d_pool), lambda b, r: (row_blk(b, r), 2)),
            pl.BlockSpec((N_GROUPS, gc, gc), lambda b, r: (0, 0, 0)),
            pl.BlockSpec((1, d_pool), lambda b, r: (0, 0)),
            pl.BlockSpec((1, d_pool), lambda b, r: (0, 0)),
            pl.BlockSpec((N_GROUPS, CHUNK, CHUNK), lambda b, r: (0, 0, 0)),
            pl.BlockSpec((N_GROUPS, CHUNK, 1), lambda b, r: (0, 0, 0)),
        ],
        out_specs=pl.BlockSpec((rows, 2 * d_pool), lambda b, r: (row_blk(b, r), 0)),
        out_shape=jax.ShapeDtypeStruct((batch * seq, 2 * d_pool), BF16),
        scratch_shapes=[pltpu.VMEM((HALO + rows, d_pool), F32), pltpu.VMEM((rows, gc), BF16)],
        compiler_params=_params("arbitrary", "arbitrary"),
        name="mix_prompt",
    )(pz, pz, pz, pz, w_pool, pool_scale.reshape(1, d_pool), g_v.reshape(1, d_pool), w_s,
      b_s.reshape(N_GROUPS, CHUNK, 1))


def _mix_sample_kernel(ws_ref, bs_ref, p_ref, st_ref, u_ref, v_ref, wp_ref, ps_ref, gv_ref,
                       op_ref, og_ref, ov_ref, pooled_ref):
    g = pl.program_id(0)
    n_t, nb, _ = p_ref.shape
    n_state = st_ref.shape[0]

    for gg, w in enumerate(POOL_WINDOWS):
        @pl.when(g == gg)
        def _(w=w):
            for t in range(n_t):
                acc = p_ref[t]
                for back in range(1, w):
                    e = n_state + t - back
                    acc = acc + (p_ref[e - n_state] if e >= n_state else st_ref[e])
                cnt = float(min(PAST_LEN + t + 1, w))
                pooled_ref[t * nb:(t + 1) * nb, :] = (acc / cnt - p_ref[t]).astype(BF16)

    mixed = _bdot(pooled_ref[...], wp_ref[0].astype(BF16)) * ps_ref[...]
    for t in range(n_t):
        op_ref[t] = mixed[t * nb:(t + 1) * nb].astype(op_ref.dtype)

    vn = []
    for t in range(n_t):
        vr = v_ref[t]
        ms = jnp.mean(vr * vr, axis=-1, keepdims=True)
        vt = vr * lax.rsqrt(ms + EPS) * gv_ref[...]
        ov_ref[t] = vt
        vn.append(vt)
    for t in range(n_t):
        m = ws_ref[g, t * n_t] * vn[0]
        for s in range(1, t + 1):
            m = m + ws_ref[g, t * n_t + s] * vn[s]
        og_ref[t] = (u_ref[t] * (m + bs_ref[g, t])).astype(og_ref.dtype)


def _mix_sample(pz_tm, state_tm, w_pool, pool_scale, g_v, w_s, b_s):
    n_t, nb, _ = pz_tm.shape
    n_state = state_tm.shape[0]
    d_pool = pool_scale.shape[-1]
    gc = d_pool // N_GROUPS
    ws_small = w_s[:, :n_t, :n_t].reshape(N_GROUPS, n_t * n_t)
    bs_small = b_s[:, :n_t]
    smem = pl.BlockSpec(memory_space=pltpu.SMEM)
    slab = lambda off: pl.BlockSpec((n_t, nb, gc), lambda g: (0, 0, g + off))
    vec = pl.BlockSpec((1, gc), lambda g: (0, g))
    return pl.pallas_call(
        _mix_sample_kernel,
        grid=(N_GROUPS,),
        in_specs=[
            smem, smem,
            slab(0),
            pl.BlockSpec((n_state, nb, gc), lambda g: (0, 0, g)),
            slab(N_GROUPS), slab(2 * N_GROUPS),
            pl.BlockSpec((1, gc, gc), lambda g: (g, 0, 0)),
            vec, vec,
        ],
        out_specs=[pl.BlockSpec((n_t, nb, gc), lambda g: (0, 0, g)) for _ in range(3)],
        out_shape=[jax.ShapeDtypeStruct((n_t, nb, d_pool), BF16),
                   jax.ShapeDtypeStruct((n_t, nb, d_pool), BF16),
                   jax.ShapeDtypeStruct((n_t, nb, d_pool), F32)],
        scratch_shapes=[pltpu.VMEM((n_t * nb, gc), BF16)],
        compiler_params=_params("arbitrary"),
        name="mix_sample",
    )(ws_small, bs_small, pz_tm, state_tm, pz_tm, pz_tm, w_pool, pool_scale.reshape(1, d_pool),
      g_v.reshape(1, d_pool))


def _softmax_rows(s):
    m = jnp.max(s, axis=-1, keepdims=True)
    e = jnp.exp(s - m)
    return e / jnp.sum(e, axis=-1, keepdims=True)


def _attn_prompt_kernel(q_ref, k_ref, v_ref, o_ref, *, scale):
    s = lax.dot_general(q_ref[...], k_ref[...], (((1,), (1,)), ((), ())),
                        preferred_element_type=F32) * scale
    a = _softmax_rows(s).astype(BF16)
    o_ref[...] = _bdot(a, v_ref[...]).astype(o_ref.dtype)


def _attn_prompt(q, k, v, *, batch, seq, n_mem, tq):
    d = q.shape[1]
    hd = d // N_GROUPS
    qb = seq // tq
    kern = functools.partial(_attn_prompt_kernel, scale=hd ** -0.5)
    return pl.pallas_call(
        kern,
        grid=(batch, N_GROUPS, qb),
        in_specs=[
            pl.BlockSpec((tq, hd), lambda b, h, i: (b * qb + i, h)),
            pl.BlockSpec((n_mem, hd), lambda b, h, i: (b, h)),
            pl.BlockSpec((n_mem, hd), lambda b, h, i: (b, h)),
        ],
        out_specs=pl.BlockSpec((tq, hd), lambda b, h, i: (b * qb + i, h)),
        out_shape=jax.ShapeDtypeStruct((batch * seq, d), BF16),
        compiler_params=_params("arbitrary", "arbitrary", "arbitrary"),
        name="attn_prompt",
    )(q, k, v)


def _attn_sample_kernel(q_ref, k_ref, v_ref, o_ref, *, scale, n_q):
    nb = k_ref.shape[0]
    hd = k_ref.shape[2] // N_GROUPS
    row_batch = lax.broadcasted_iota(jnp.int32, (nb * n_q, 1), 0) // n_q
    for h in range(N_GROUPS):
        cols = slice(h * hd, (h + 1) * hd)
        qh = q_ref[:, cols].astype(BF16)
        out = jnp.zeros((nb * n_q, hd), F32)
        for b in range(nb):
            kb = k_ref[b, :, cols].astype(BF16)
            vb = v_ref[b, :, cols].astype(BF16)
            s = lax.dot_general(qh, kb, (((1,), (1,)), ((), ())), preferred_element_type=F32) * scale
            a = _softmax_rows(s).astype(BF16)
            out = jnp.where(row_batch == b, _bdot(a, vb), out)
        o_ref[:, cols] = out.astype(o_ref.dtype)


def _attn_sample(q, mem_k, mem_v, *, n_q, nb):
    batch, n_mem, d = mem_k.shape
    kern = functools.partial(_attn_sample_kernel, scale=(d // N_GROUPS) ** -0.5, n_q=n_q)
    return pl.pallas_call(
        kern,
        grid=(batch // nb,),
        in_specs=[
            pl.BlockSpec((nb * n_q, d), lambda i: (i, 0)),
            pl.BlockSpec((nb, n_mem, d), lambda i: (i, 0, 0)),
            pl.BlockSpec((nb, n_mem, d), lambda i: (i, 0, 0)),
        ],
        out_specs=pl.BlockSpec((nb * n_q, d), lambda i: (i, 0)),
        out_shape=jax.ShapeDtypeStruct((batch * n_q, d), F32),
        compiler_params=_params("arbitrary"),
        name="attn_sample",
    )(q, mem_k, mem_v)


def _layer(x, mixers, attention, lw, *, tm, tag):
    (g_mix, w_in, w_out, g_xattn, w_q, w_o, g_ffn, w_gate_up, w_down, d_pool) = lw
    pz = _norm_mm(x, g_mix, w_in, tm=tm, tn=512, out_dtypes=(F32,), gelu_from_col=d_pool,
                  name="in_proj_" + tag)[0]
    mix, extras = mixers(pz)
    x = _mm_res(mix, w_out, x, tm=tm, tn=512, name="out_proj_" + tag)
    q = _norm_mm(x, g_xattn, w_q, tm=tm, tn=512, out_dtypes=attention.q_dtypes, name="q_proj_" + tag)[0]
    o = attention(q)
    x = _mm_res(o.astype(BF16), w_o, x, tm=tm, tn=512, name="o_proj_" + tag)
    act = _swiglu(x, g_ffn, w_gate_up, tm=tm, tn=256, name="ffn_up_" + tag)
    x = _mm_res(act, w_down, x, tm=tm, tn=256, name="ffn_down_" + tag)
    return x, pz, extras


def kernel(x_prompt, x_sample, cache_mem_k, cache_mem_v, state_pool, mem_prompt, g_mix, w_in, g_v, w_pool,
           pool_scale, w_s, b_s, w_out, g_xattn, g_mem, w_q, w_k, w_v, w_o, g_ffn, w_gate_up, w_down, g_final):
    depth = g_mix.shape[0]
    assert depth == 1, "single-layer step"
    batch, seq, d = x_prompt.shape
    dec_batch, dec_seq, _ = x_sample.shape
    n_mem = mem_prompt.shape[1]
    d_pool = pool_scale.shape[-1]
    l = 0
    lw = (g_mix[l], w_in[l], w_out[l], g_xattn[l], w_q[l], w_o[l], g_ffn[l], w_gate_up[l], w_down[l], d_pool)

    mem = mem_prompt.reshape(batch * n_mem, d)
    mk, mk_b = _norm_mm(mem, g_mem[l], w_k[l], tm=batch * n_mem, tn=512, out_dtypes=(F32, BF16), name="mem_k")
    mv, mv_b = _norm_mm(mem, g_mem[l], w_v[l], tm=batch * n_mem, tn=512, out_dtypes=(F32, BF16), name="mem_v")

    def prompt_mixers(pz):
        return _mix_prompt(pz, w_pool[l], pool_scale[l], g_v[l], w_s[l], b_s[l],
                           batch=batch, seq=seq, rows=512), None

    def prompt_attention(q):
        return _attn_prompt(q, mk_b, mv_b, batch=batch, seq=seq, n_mem=n_mem, tq=1024)
    prompt_attention.q_dtypes = (BF16,)

    xp, pz_p, _ = _layer(x_prompt.reshape(batch * seq, d), prompt_mixers, prompt_attention, lw,
                         tm=1024, tag="prompt")
    y_prompt = _rmsnorm(xp, g_final, tm=512, name="final_norm_prompt").reshape(batch, seq, d)
    n_state = state_pool.shape[2]
    pool_p = pz_p.reshape(batch, seq, -1)[:, seq - n_state:, :d_pool]

    def sample_mixers(pz):
        pz_tm = pz.reshape(dec_batch, dec_seq, -1).transpose(1, 0, 2)
        state_tm = state_pool[l].transpose(1, 0, 2)
        mp, mg, v_tm = _mix_sample(pz_tm, state_tm, w_pool[l], pool_scale[l], g_v[l], w_s[l], b_s[l])
        mix = jnp.concatenate([mp, mg], axis=-1).transpose(1, 0, 2).reshape(dec_batch * dec_seq, -1)
        return mix, v_tm.transpose(1, 0, 2)

    def sample_attention(q):
        return _attn_sample(q, cache_mem_k[l].reshape(dec_batch, n_mem, d),
                            cache_mem_v[l].reshape(dec_batch, n_mem, d), n_q=dec_seq, nb=2)
    sample_attention.q_dtypes = (F32,)

    xs, pz_s, v_s = _layer(x_sample.reshape(dec_batch * dec_seq, d), sample_mixers, sample_attention, lw,
                           tm=dec_batch * dec_seq, tag="sample")
    y_sample = _rmsnorm(xs, g_final, tm=512, name="final_norm_sample").reshape(dec_batch, dec_seq, d)
    p_s = pz_s[:, :d_pool].reshape(dec_batch, dec_seq, d_pool)
    pool_s = jnp.concatenate([state_pool[l], p_s], axis=1)[:, -n_state:]

    hd = d // N_GROUPS
    return (y_prompt, y_sample,
            mk.reshape(1, batch, n_mem, N_GROUPS, hd), mv.reshape(1, batch, n_mem, N_GROUPS, hd),
            pool_p[None], pool_s[None], v_s[None])
```

```python
import functools
import math

import jax
import jax.numpy as jnp
from jax import lax
from jax.experimental import pallas as pl
from jax.experimental.pallas import tpu as pltpu

EPS = 1e-6
POOL_WINDOWS = (2, 4, 8, 16)
N_GROUPS = 4
CHUNK = 128
HALO = 16
PAST_LEN = 16384
MASKED_SCORE = -1e30

V7X_VMEM_LIMIT_BYTES = 60000 * 1024

BF16 = jnp.bfloat16
F32 = jnp.float32


def _params(*sem):
    return pltpu.CompilerParams(dimension_semantics=sem, vmem_limit_bytes=V7X_VMEM_LIMIT_BYTES)


def _bdot(a, b):
    return jnp.dot(a, b, preferred_element_type=F32)


def _norm_rows_into(x_ref, g_ref, h_ref, rows=64):
    def body(c, carry):
        r0 = pl.multiple_of(c * rows, rows)
        x = x_ref[pl.ds(r0, rows), :]
        ms = jnp.mean(x * x, axis=-1, keepdims=True)
        h_ref[pl.ds(r0, rows), :] = (x * lax.rsqrt(ms + EPS) * g_ref[...]).astype(h_ref.dtype)
        return carry
    lax.fori_loop(0, x_ref.shape[0] // rows, body, 0)


def _gelu(x):
    return 0.5 * x * (1.0 + lax.erf(x * math.sqrt(0.5)))


def _norm_mm_kernel(x_ref, g_ref, w_ref, *refs, n_out, gelu_from):
    outs, h_ref = refs[:n_out], refs[n_out]
    j = pl.program_id(1)

    @pl.when(j == 0)
    def _():
        _norm_rows_into(x_ref, g_ref, h_ref)

    acc = _bdot(h_ref[...], w_ref[...].astype(BF16))
    if gelu_from is None:
        for o in outs:
            o[...] = acc.astype(o.dtype)
    else:
        @pl.when(j < gelu_from)
        def _():
            outs[0][...] = acc

        @pl.when(j >= gelu_from)
        def _():
            outs[0][...] = _gelu(acc)


def _norm_mm(x, g, w, *, tm, tn, out_dtypes, gelu_from_col=None, name):
    m, k = x.shape
    n = w.shape[1]
    kern = functools.partial(_norm_mm_kernel, n_out=len(out_dtypes),
                             gelu_from=None if gelu_from_col is None else gelu_from_col // tn)
    return pl.pallas_call(
        kern,
        grid=(m // tm, n // tn),
        in_specs=[
            pl.BlockSpec((tm, k), lambda i, j: (i, 0), pipeline_mode=pl.Buffered(1)),
            pl.BlockSpec((1, k), lambda i, j: (0, 0)),
            pl.BlockSpec((k, tn), lambda i, j: (0, j)),
        ],
        out_specs=[pl.BlockSpec((tm, tn), lambda i, j: (i, j)) for _ in out_dtypes],
        out_shape=[jax.ShapeDtypeStruct((m, n), d) for d in out_dtypes],
        scratch_shapes=[pltpu.VMEM((tm, k), BF16)],
        compiler_params=_params("arbitrary", "arbitrary"),
        name=name,
    )(x, g.reshape(1, k), w)


def _swiglu_kernel(x_ref, g_ref, wg_ref, wu_ref, o_ref, h_ref):
    @pl.when(pl.program_id(1) == 0)
    def _():
        _norm_rows_into(x_ref, g_ref, h_ref)

    h = h_ref[...]
    gate = _bdot(h, wg_ref[...].astype(BF16))
    up = _bdot(h, wu_ref[...].astype(BF16))
    o_ref[...] = (gate * (1.0 / (1.0 + jnp.exp(-gate))) * up).astype(o_ref.dtype)


def _swiglu(x, g, w_gate_up, *, tm, tn, name):
    m, k = x.shape
    d_ff = w_gate_up.shape[1] // 2
    nt = d_ff // tn
    return pl.pallas_call(
        _swiglu_kernel,
        grid=(m // tm, nt),
        in_specs=[
            pl.BlockSpec((tm, k), lambda i, j: (i, 0), pipeline_mode=pl.Buffered(1)),
            pl.BlockSpec((1, k), lambda i, j: (0, 0)),
            pl.BlockSpec((k, tn), lambda i, j: (0, j)),
            pl.BlockSpec((k, tn), lambda i, j: (0, j + nt)),
        ],
        out_specs=pl.BlockSpec((tm, tn), lambda i, j: (i, j)),
        out_shape=jax.ShapeDtypeStruct((m, d_ff), BF16),
        scratch_shapes=[pltpu.VMEM((tm, k), BF16)],
        compiler_params=_params("arbitrary", "arbitrary"),
        name=name,
    )(x, g.reshape(1, k), w_gate_up, w_gate_up)


def _mm_res_kernel(a_ref, w_ref, r_ref, o_ref):
    o_ref[...] = r_ref[...] + _bdot(a_ref[...], w_ref[...].astype(BF16))


def _mm_res(a, w, res, *, tm, tn, a_buffers, name):
    m, k = a.shape
    n = w.shape[1]
    return pl.pallas_call(
        _mm_res_kernel,
        grid=(m // tm, n // tn),
        in_specs=[
            pl.BlockSpec((tm, k), lambda i, j: (i, 0), pipeline_mode=pl.Buffered(a_buffers)),
            pl.BlockSpec((k, tn), lambda i, j: (0, j)),
            pl.BlockSpec((tm, tn), lambda i, j: (i, j)),
        ],
        out_specs=pl.BlockSpec((tm, tn), lambda i, j: (i, j)),
        out_shape=jax.ShapeDtypeStruct((m, n), F32),
        compiler_params=_params("arbitrary", "arbitrary"),
        name=name,
    )(a, w, res)


def _rmsnorm_kernel(x_ref, g_ref, o_ref):
    x = x_ref[...]
    ms = jnp.mean(x * x, axis=-1, keepdims=True)
    o_ref[...] = x * lax.rsqrt(ms + EPS) * g_ref[...]


def _rmsnorm(x, g, *, tm, name):
    m, k = x.shape
    return pl.pallas_call(
        _rmsnorm_kernel,
        grid=(m // tm,),
        in_specs=[pl.BlockSpec((tm, k), lambda i: (i, 0)), pl.BlockSpec((1, k), lambda i: (0, 0))],
        out_specs=pl.BlockSpec((tm, k), lambda i: (i, 0)),
        out_shape=jax.ShapeDtypeStruct((m, k), F32),
        compiler_params=_params("arbitrary"),
        name=name,
    )(x, g.reshape(1, k))


def _mix_prompt_kernel(p_ref, halo_ref, u_ref, v_ref, wp_ref, ps_ref, gv_ref, ws_ref, bs_ref,
                       o_ref, ext_ref, pooled_ref, *, rows):
    r = pl.program_id(1)
    d_pool = p_ref.shape[1]
    gc = d_pool // N_GROUPS

    @pl.when(r == 0)
    def _():
        ext_ref[0:HALO, :] = jnp.zeros((HALO, d_pool), F32)

    @pl.when(r > 0)
    def _():
        ext_ref[0:HALO, :] = halo_ref[...]

    ext_ref[HALO:HALO + rows, :] = p_ref[...]

    pos = r * rows + lax.broadcasted_iota(jnp.int32, (CHUNK, 1), 0)
    for g, w in enumerate(POOL_WINDOWS):
        cols = slice(g * gc, (g + 1) * gc)
        for c in range(rows // CHUNK):
            base = HALO + c * CHUNK
            tok = ext_ref[base:base + CHUNK, cols]
            acc = tok
            for back in range(1, w):
                acc = acc + ext_ref[base - back:base - back + CHUNK, cols]
            cnt = jnp.minimum(pos + (c * CHUNK + 1), w).astype(F32)
            pooled_ref[c * CHUNK:(c + 1) * CHUNK, :] = (acc / cnt - tok).astype(BF16)
        mixed = _bdot(pooled_ref[...], wp_ref[g].astype(BF16)) * ps_ref[:, cols]
        o_ref[:, cols] = mixed.astype(o_ref.dtype)

    row_id = lax.broadcasted_iota(jnp.int32, (CHUNK, CHUNK), 0)
    col_id = lax.broadcasted_iota(jnp.int32, (CHUNK, CHUNK), 1)
    for h in range(N_GROUPS):
        cols = slice(h * gc, (h + 1) * gc)
        w_tri = jnp.where(row_id >= col_id, ws_ref[h], 0.0).astype(BF16)
        bias = bs_ref[h]
        for c in range(rows // CHUNK):
            rs = slice(c * CHUNK, (c + 1) * CHUNK)
            vr = v_ref[rs, cols]
            ms = jnp.mean(vr * vr, axis=-1, keepdims=True)
            vn = (vr * lax.rsqrt(ms + EPS) * gv_ref[:, cols]).astype(BF16)
            mixed = _bdot(w_tri, vn) + bias
            o_ref[rs, d_pool + h * gc:d_pool + (h + 1) * gc] = (u_ref[rs, cols] * mixed).astype(o_ref.dtype)


def _mix_prompt(pz, w_pool, pool_scale, g_v, w_s, b_s, *, batch, seq, rows):
    d_pool = pool_scale.shape[-1]
    gc = d_pool // N_GROUPS
    blocks_per_seq = seq // rows

    def row_blk(b, r):
        return b * blocks_per_seq + r

    def halo_blk(b, r):
        return jnp.maximum((b * seq + r * rows) // HALO - 1, 0)

    kern = functools.partial(_mix_prompt_kernel, rows=rows)
    return pl.pallas_call(
        kern,
        grid=(batch, blocks_per_seq),
        in_specs=[
            pl.BlockSpec((rows, d_pool), lambda b, r: (row_blk(b, r), 0)),
            pl.BlockSpec((HALO, d_pool), lambda b, r: (halo_blk(b, r), 0)),
            pl.BlockSpec((rows, d_pool), lambda b, r: (row_blk(b, r), 1)),
            pl.BlockSpec((rows, d_pool), lambda b, r: (row_blk(b, r), 2)),
            pl.BlockSpec((N_GROUPS, gc, gc), lambda b, r: (0, 0, 0)),
            pl.BlockSpec((1, d_pool), lambda b, r: (0, 0)),
            pl.BlockSpec((1, d_pool), lambda b, r: (0, 0)),
            pl.BlockSpec((N_GROUPS, CHUNK, CHUNK), lambda b, r: (0, 0, 0)),
            pl.BlockSpec((N_GROUPS, CHUNK, 1), lambda b, r: (0, 0, 0)),
        ],
        out_specs=pl.BlockSpec((rows, 2 * d_pool), lambda b, r: (row_blk(b, r), 0)),
        out_shape=jax.ShapeDtypeStruct((batch * seq, 2 * d_pool), BF16),
        scratch_shapes=[pltpu.VMEM((HALO + rows, d_pool), F32), pltpu.VMEM((rows, gc), BF16)],
        compiler_params=_params("arbitrary", "arbitrary"),
        name="mix_prompt",
    )(pz, pz, pz, pz, w_pool, pool_scale.reshape(1, d_pool), g_v.reshape(1, d_pool), w_s,
      b_s.reshape(N_GROUPS, CHUNK, 1))


def _mix_sample_kernel(ws_ref, bs_ref, p_ref, st_ref, u_ref, v_ref, wp_ref, ps_ref, gv_ref,
                       op_ref, og_ref, ov_ref, pooled_ref):
    g = pl.program_id(0)
    n_t, nb, _ = p_ref.shape
    n_state = st_ref.shape[0]

    for gg, w in enumerate(POOL_WINDOWS):
        @pl.when(g == gg)
        def _(w=w):
            for t in range(n_t):
                acc = p_ref[t]
                for back in range(1, w):
                    e = n_state + t - back
                    acc = acc + (p_ref[e - n_state] if e >= n_state else st_ref[e])
                cnt = float(min(PAST_LEN + t + 1, w))
                pooled_ref[t * nb:(t + 1) * nb, :] = (acc / cnt - p_ref[t]).astype(BF16)

    mixed = _bdot(pooled_ref[...], wp_ref[0].astype(BF16)) * ps_ref[...]
    for t in range(n_t):
        op_ref[t] = mixed[t * nb:(t + 1) * nb].astype(op_ref.dtype)

    vn = []
    for t in range(n_t):
        vr = v_ref[t]
        ms = jnp.mean(vr * vr, axis=-1, keepdims=True)
        vt = vr * lax.rsqrt(ms + EPS) * gv_ref[...]
        ov_ref[t] = vt
        vn.append(vt)
    for t in range(n_t):
        m = ws_ref[g, t * n_t] * vn[0]
        for s in range(1, t + 1):
            m = m + ws_ref[g, t * n_t + s] * vn[s]
        og_ref[t] = (u_ref[t] * (m + bs_ref[g, t])).astype(og_ref.dtype)


def _mix_sample(pz_tm, state_tm, w_pool, pool_scale, g_v, w_s, b_s):
    n_t, nb, _ = pz_tm.shape
    n_state = state_tm.shape[0]
    d_pool = pool_scale.shape[-1]
    gc = d_pool // N_GROUPS
    ws_small = w_s[:, :n_t, :n_t].reshape(N_GROUPS, n_t * n_t)
    bs_small = b_s[:, :n_t]
    smem = pl.BlockSpec(memory_space=pltpu.SMEM)
    slab = lambda off: pl.BlockSpec((n_t, nb, gc), lambda g: (0, 0, g + off))
    vec = pl.BlockSpec((1, gc), lambda g: (0, g))
    return pl.pallas_call(
        _mix_sample_kernel,
        grid=(N_GROUPS,),
        in_specs=[
            smem, smem,
            slab(0),
            pl.BlockSpec((n_state, nb, gc), lambda g: (0, 0, g)),
            slab(N_GROUPS), slab(2 * N_GROUPS),
            pl.BlockSpec((1, gc, gc), lambda g: (g, 0, 0)),
            vec, vec,
        ],
        out_specs=[pl.BlockSpec((n_t, nb, gc), lambda g: (0, 0, g)) for _ in range(3)],
        out_shape=[jax.ShapeDtypeStruct((n_t, nb, d_pool), BF16),
                   jax.ShapeDtypeStruct((n_t, nb, d_pool), BF16),
                   jax.ShapeDtypeStruct((n_t, nb, d_pool), F32)],
        scratch_shapes=[pltpu.VMEM((n_t * nb, gc), BF16)],
        compiler_params=_params("arbitrary"),
        name="mix_sample",
    )(ws_small, bs_small, pz_tm, state_tm, pz_tm, pz_tm, w_pool, pool_scale.reshape(1, d_pool),
      g_v.reshape(1, d_pool))


def _softmax_rows(s):
    m = jnp.max(s, axis=-1, keepdims=True)
    e = jnp.exp(s - m)
    return e / jnp.sum(e, axis=-1, keepdims=True)


def _attn_prompt_kernel(q_ref, k_ref, v_ref, o_ref, *, scale):
    s = lax.dot_general(q_ref[...], k_ref[...], (((1,), (1,)), ((), ())),
                        preferred_element_type=F32) * scale
    a = _softmax_rows(s).astype(BF16)
    o_ref[...] = _bdot(a, v_ref[...]).astype(o_ref.dtype)


def _attn_prompt(q, k, v, *, batch, seq, n_mem, tq):
    d = q.shape[1]
    hd = d // N_GROUPS
    qb = seq // tq
    kern = functools.partial(_attn_prompt_kernel, scale=hd ** -0.5)
    return pl.pallas_call(
        kern,
        grid=(batch, N_GROUPS, qb),
        in_specs=[
            pl.BlockSpec((tq, hd), lambda b, h, i: (b * qb + i, h)),
            pl.BlockSpec((n_mem, hd), lambda b, h, i: (b, h)),
            pl.BlockSpec((n_mem, hd), lambda b, h, i: (b, h)),
        ],
        out_specs=pl.BlockSpec((tq, hd), lambda b, h, i: (b * qb + i, h)),
        out_shape=jax.ShapeDtypeStruct((batch * seq, d), BF16),
        compiler_params=_params("arbitrary", "arbitrary", "arbitrary"),
        name="attn_prompt",
    )(q, k, v)


def _attn_sample_kernel(q_ref, k_ref, v_ref, o_ref, *, scale, n_q):
    _, nb, n_mem, n_heads, hd = k_ref.shape
    rows = n_q * n_heads
    r_head = lax.broadcasted_iota(jnp.int32, (rows, n_mem * n_heads), 0) % n_heads
    c_head = lax.broadcasted_iota(jnp.int32, (rows, n_mem * n_heads), 1) % n_heads
    same_head = r_head == c_head
    for b in range(nb):
        rs = slice(b * rows, (b + 1) * rows)
        qb = q_ref[rs, :].astype(BF16)
        km = k_ref[0, b].reshape(n_mem * n_heads, hd).astype(BF16)
        vm = v_ref[0, b].reshape(n_mem * n_heads, hd).astype(BF16)
        s = lax.dot_general(qb, km, (((1,), (1,)), ((), ())), preferred_element_type=F32) * scale
        a = _softmax_rows(jnp.where(same_head, s, MASKED_SCORE)).astype(BF16)
        o_ref[rs, :] = _bdot(a, vm)


def _attn_sample(q, mem_k, mem_v, *, layer, n_q, nb):
    _, batch, n_mem, n_heads, hd = mem_k.shape
    rows = n_q * n_heads
    kern = functools.partial(_attn_sample_kernel, scale=hd ** -0.5, n_q=n_q)
    mem_spec = pl.BlockSpec((1, nb, n_mem, n_heads, hd), lambda i: (layer, i, 0, 0, 0))
    return pl.pallas_call(
        kern,
        grid=(batch // nb,),
        in_specs=[pl.BlockSpec((nb * rows, hd), lambda i: (i, 0)), mem_spec, mem_spec],
        out_specs=pl.BlockSpec((nb * rows, hd), lambda i: (i, 0)),
        out_shape=jax.ShapeDtypeStruct((batch * rows, hd), F32),
        compiler_params=_params("arbitrary"),
        name="attn_sample",
    )(q, mem_k, mem_v)


def _layer(x, mixers, attention, lw, *, tm, tag):
    (g_mix, w_in, w_out, g_xattn, w_q, w_o, g_ffn, w_gate_up, w_down, d_pool) = lw
    pz = _norm_mm(x, g_mix, w_in, tm=tm, tn=512, out_dtypes=(F32,), gelu_from_col=d_pool,
                  name="in_proj_" + tag)[0]
    mix, extras = mixers(pz)
    x = _mm_res(mix, w_out, x, tm=tm, tn=512, a_buffers=2, name="out_proj_" + tag)
    q = _norm_mm(x, g_xattn, w_q, tm=tm, tn=512, out_dtypes=attention.q_dtypes, name="q_proj_" + tag)[0]
    o = attention(q)
    x = _mm_res(o.astype(BF16), w_o, x, tm=tm, tn=512, a_buffers=2, name="o_proj_" + tag)
    act = _swiglu(x, g_ffn, w_gate_up, tm=tm, tn=256, name="ffn_up_" + tag)
    x = _mm_res(act, w_down, x, tm=tm, tn=256, a_buffers=1, name="ffn_down_" + tag)
    return x, pz, extras


def kernel(x_prompt, x_sample, cache_mem_k, cache_mem_v, state_pool, mem_prompt, g_mix, w_in, g_v, w_pool,
           pool_scale, w_s, b_s, w_out, g_xattn, g_mem, w_q, w_k, w_v, w_o, g_ffn, w_gate_up, w_down, g_final):
    depth = g_mix.shape[0]
    assert depth == 1, "single-layer step"
    batch, seq, d = x_prompt.shape
    dec_batch, dec_seq, _ = x_sample.shape
    n_mem = mem_prompt.shape[1]
    d_pool = pool_scale.shape[-1]
    l = 0
    lw = (g_mix[l], w_in[l], w_out[l], g_xattn[l], w_q[l], w_o[l], g_ffn[l], w_gate_up[l], w_down[l], d_pool)

    mem = mem_prompt.reshape(batch * n_mem, d)
    mk, mk_b = _norm_mm(mem, g_mem[l], w_k[l], tm=batch * n_mem, tn=512, out_dtypes=(F32, BF16), name="mem_k")
    mv, mv_b = _norm_mm(mem, g_mem[l], w_v[l], tm=batch * n_mem, tn=512, out_dtypes=(F32, BF16), name="mem_v")

    def prompt_mixers(pz):
        return _mix_prompt(pz, w_pool[l], pool_scale[l], g_v[l], w_s[l], b_s[l],
                           batch=batch, seq=seq, rows=512), None

    def prompt_attention(q):
        return _attn_prompt(q, mk_b, mv_b, batch=batch, seq=seq, n_mem=n_mem, tq=1024)
    prompt_attention.q_dtypes = (BF16,)

    xp, pz_p, _ = _layer(x_prompt.reshape(batch * seq, d), prompt_mixers, prompt_attention, lw,
                         tm=1024, tag="prompt")
    y_prompt = _rmsnorm(xp, g_final, tm=512, name="final_norm_prompt").reshape(batch, seq, d)
    n_state = state_pool.shape[2]
    pool_p = pz_p.reshape(batch, seq, -1)[:, seq - n_state:, :d_pool]

    def sample_mixers(pz):
        pz_tm = pz.reshape(dec_batch, dec_seq, -1).transpose(1, 0, 2)
        state_tm = state_pool[l].transpose(1, 0, 2)
        mp, mg, v_tm = _mix_sample(pz_tm, state_tm, w_pool[l], pool_scale[l], g_v[l], w_s[l], b_s[l])
        mix = jnp.concatenate([mp, mg], axis=-1).transpose(1, 0, 2).reshape(dec_batch * dec_seq, -1)
        return mix, v_tm.transpose(1, 0, 2)

    def sample_attention(q):
        n_heads, hd = cache_mem_k.shape[-2:]
        o = _attn_sample(q.reshape(-1, hd), cache_mem_k, cache_mem_v, layer=l, n_q=dec_seq, nb=2)
        return o.reshape(dec_batch * dec_seq, n_heads * hd)
    sample_attention.q_dtypes = (F32,)

    xs, pz_s, v_s = _layer(x_sample.reshape(dec_batch * dec_seq, d), sample_mixers, sample_attention, lw,
                           tm=dec_batch * dec_seq, tag="sample")
    y_sample = _rmsnorm(xs, g_final, tm=512, name="final_norm_sample").reshape(dec_batch, dec_seq, d)
    p_s = pz_s[:, :d_pool].reshape(dec_batch, dec_seq, d_pool)
    pool_s = jnp.concatenate([state_pool[l], p_s], axis=1)[:, -n_state:]

    hd = d // N_GROUPS
    return (y_prompt, y_sample,
            mk.reshape(1, batch, n_mem, N_GROUPS, hd), mv.reshape(1, batch, n_mem, N_GROUPS, hd),
            pool_p[None], pool_s[None], v_s[None])
```

```python
import functools
import math

import jax
import jax.numpy as jnp
from jax import lax
from jax.experimental import pallas as pl
from jax.experimental.pallas import tpu as pltpu

EPS = 1e-6
POOL_WINDOWS = (2, 4, 8, 16)
N_GROUPS = 4
CHUNK = 128
HALO = 16
PAST_LEN = 16384
MASKED_SCORE = -1e30

V7X_VMEM_LIMIT_BYTES = 60000 * 1024

BF16 = jnp.bfloat16
F32 = jnp.float32


def _params(*sem):
    return pltpu.CompilerParams(dimension_semantics=sem, vmem_limit_bytes=V7X_VMEM_LIMIT_BYTES)


def _bdot(a, b):
    return jnp.dot(a, b, preferred_element_type=F32)


def _norm_rows_into(x_ref, g_ref, h_ref, rows=64):
    def body(c, carry):
        r0 = pl.multiple_of(c * rows, rows)
        x = x_ref[pl.ds(r0, rows), :]
        ms = jnp.mean(x * x, axis=-1, keepdims=True)
        h_ref[pl.ds(r0, rows), :] = (x * lax.rsqrt(ms + EPS) * g_ref[...]).astype(h_ref.dtype)
        return carry
    lax.fori_loop(0, x_ref.shape[0] // rows, body, 0)


def _gelu(x):
    return 0.5 * x * (1.0 + lax.erf(x * math.sqrt(0.5)))


def _norm_mm_kernel(x_ref, g_ref, w_ref, *refs, n_out, gelu_from):
    outs, h_ref = refs[:n_out], refs[n_out]
    j = pl.program_id(1)

    @pl.when(j == 0)
    def _():
        _norm_rows_into(x_ref, g_ref, h_ref)

    acc = _bdot(h_ref[...], w_ref[...].astype(BF16))
    if gelu_from is None:
        for o in outs:
            o[...] = acc.astype(o.dtype)
    else:
        @pl.when(j < gelu_from)
        def _():
            outs[0][...] = acc

        @pl.when(j >= gelu_from)
        def _():
            outs[0][...] = _gelu(acc)


def _norm_mm(x, g, w, *, tm, tn, out_dtypes, gelu_from_col=None, name):
    m, k = x.shape
    n = w.shape[1]
    kern = functools.partial(_norm_mm_kernel, n_out=len(out_dtypes),
                             gelu_from=None if gelu_from_col is None else gelu_from_col // tn)
    return pl.pallas_call(
        kern,
        grid=(m // tm, n // tn),
        in_specs=[
            pl.BlockSpec((tm, k), lambda i, j: (i, 0), pipeline_mode=pl.Buffered(1)),
            pl.BlockSpec((1, k), lambda i, j: (0, 0)),
            pl.BlockSpec((k, tn), lambda i, j: (0, j)),
        ],
        out_specs=[pl.BlockSpec((tm, tn), lambda i, j: (i, j)) for _ in out_dtypes],
        out_shape=[jax.ShapeDtypeStruct((m, n), d) for d in out_dtypes],
        scratch_shapes=[pltpu.VMEM((tm, k), BF16)],
        compiler_params=_params("arbitrary", "arbitrary"),
        name=name,
    )(x, g.reshape(1, k), w)


def _scaled_mm_kernel(a_ref, rs_ref, w_ref, *outs):
    acc = _bdot(a_ref[...], w_ref[...].astype(BF16)) * rs_ref[...]
    for o in outs:
        o[...] = acc.astype(o.dtype)


def _scaled_mm(xg, rs, w, *, tm, tn, out_dtypes, name):
    m, k = xg.shape
    n = w.shape[1]
    return pl.pallas_call(
        _scaled_mm_kernel,
        grid=(m // tm, n // tn),
        in_specs=[
            pl.BlockSpec((tm, k), lambda i, j: (i, 0)),
            pl.BlockSpec((tm, 1), lambda i, j: (i, 0)),
            pl.BlockSpec((k, tn), lambda i, j: (0, j)),
        ],
        out_specs=[pl.BlockSpec((tm, tn), lambda i, j: (i, j)) for _ in out_dtypes],
        out_shape=[jax.ShapeDtypeStruct((m, n), d) for d in out_dtypes],
        compiler_params=_params("arbitrary", "arbitrary"),
        name=name,
    )(xg, rs, w)


def _scaled_swiglu_kernel(a_ref, rs_ref, wg_ref, wu_ref, o_ref):
    a = a_ref[...]
    rs = rs_ref[...]
    gate = _bdot(a, wg_ref[...].astype(BF16)) * rs
    up = _bdot(a, wu_ref[...].astype(BF16)) * rs
    o_ref[...] = (gate * (1.0 / (1.0 + jnp.exp(-gate))) * up).astype(o_ref.dtype)


def _scaled_swiglu(xg, rs, w_gate_up, *, tm, tn, name):
    m, k = xg.shape
    d_ff = w_gate_up.shape[1] // 2
    nt = d_ff // tn
    return pl.pallas_call(
        _scaled_swiglu_kernel,
        grid=(m // tm, nt),
        in_specs=[
            pl.BlockSpec((tm, k), lambda i, j: (i, 0)),
            pl.BlockSpec((tm, 1), lambda i, j: (i, 0)),
            pl.BlockSpec((k, tn), lambda i, j: (0, j)),
            pl.BlockSpec((k, tn), lambda i, j: (0, j + nt)),
        ],
        out_specs=pl.BlockSpec((tm, tn), lambda i, j: (i, j)),
        out_shape=jax.ShapeDtypeStruct((m, d_ff), BF16),
        compiler_params=_params("arbitrary", "arbitrary"),
        name=name,
    )(xg, rs, w_gate_up, w_gate_up)


def _mm_res_kernel(a_ref, w_ref, r_ref, *rest, n_cols):
    x = r_ref[...] + _bdot(a_ref[...], w_ref[...].astype(BF16))
    if len(rest) == 1:
        rest[0][...] = x
        return
    g_ref, o_ref, xg_ref, rs_ref, ssq_ref = rest
    j = pl.program_id(1)
    o_ref[...] = x
    xg_ref[...] = (x * g_ref[...]).astype(xg_ref.dtype)
    part = jnp.sum(x * x, axis=-1, keepdims=True)

    @pl.when(j == 0)
    def _():
        ssq_ref[...] = part

    @pl.when(j > 0)
    def _():
        ssq_ref[...] += part

    @pl.when(j == pl.num_programs(1) - 1)
    def _():
        rs_ref[...] = lax.rsqrt(ssq_ref[...] / n_cols + EPS)


def _mm_res(a, w, res, *, tm, tn, a_buffers, name, g_next=None):
    m, k = a.shape
    n = w.shape[1]
    tile = pl.BlockSpec((tm, tn), lambda i, j: (i, j))
    in_specs = [
        pl.BlockSpec((tm, k), lambda i, j: (i, 0), pipeline_mode=pl.Buffered(a_buffers)),
        pl.BlockSpec((k, tn), lambda i, j: (0, j)),
        tile,
    ]
    args = [a, w, res]
    out_specs, out_shape, scratch = tile, jax.ShapeDtypeStruct((m, n), F32), []
    if g_next is not None:
        in_specs.append(pl.BlockSpec((1, tn), lambda i, j: (0, j)))
        args.append(g_next.reshape(1, n))
        out_specs = [tile, tile, pl.BlockSpec((tm, 1), lambda i, j: (i, 0))]
        out_shape = [out_shape, jax.ShapeDtypeStruct((m, n), BF16), jax.ShapeDtypeStruct((m, 1), F32)]
        scratch = [pltpu.VMEM((tm, 1), F32)]
    return pl.pallas_call(
        functools.partial(_mm_res_kernel, n_cols=n),
        grid=(m // tm, n // tn),
        in_specs=in_specs,
        out_specs=out_specs,
        out_shape=out_shape,
        scratch_shapes=scratch,
        compiler_params=_params("arbitrary", "arbitrary"),
        name=name,
    )(*args)


def _rmsnorm_kernel(x_ref, g_ref, o_ref):
    x = x_ref[...]
    ms = jnp.mean(x * x, axis=-1, keepdims=True)
    o_ref[...] = x * lax.rsqrt(ms + EPS) * g_ref[...]


def _rmsnorm(x, g, *, tm, name):
    m, k = x.shape
    return pl.pallas_call(
        _rmsnorm_kernel,
        grid=(m // tm,),
        in_specs=[pl.BlockSpec((tm, k), lambda i: (i, 0)), pl.BlockSpec((1, k), lambda i: (0, 0))],
        out_specs=pl.BlockSpec((tm, k), lambda i: (i, 0)),
        out_shape=jax.ShapeDtypeStruct((m, k), F32),
        compiler_params=_params("arbitrary"),
        name=name,
    )(x, g.reshape(1, k))


def _mix_prompt_kernel(p_ref, halo_ref, u_ref, v_ref, wp_ref, ps_ref, gv_ref, ws_ref, bs_ref,
                       o_ref, ext_ref, pooled_ref, *, rows):
    r = pl.program_id(1)
    d_pool = p_ref.shape[1]
    gc = d_pool // N_GROUPS

    @pl.when(r == 0)
    def _():
        ext_ref[0:HALO, :] = jnp.zeros((HALO, d_pool), F32)

    @pl.when(r > 0)
    def _():
        ext_ref[0:HALO, :] = halo_ref[...]

    ext_ref[HALO:HALO + rows, :] = p_ref[...]

    pos = r * rows + lax.broadcasted_iota(jnp.int32, (CHUNK, 1), 0)
    for g, w in enumerate(POOL_WINDOWS):
        cols = slice(g * gc, (g + 1) * gc)
        for c in range(rows // CHUNK):
            base = HALO + c * CHUNK
            tok = ext_ref[base:base + CHUNK, cols]
            acc = tok
            for back in range(1, w):
                acc = acc + ext_ref[base - back:base - back + CHUNK, cols]
            cnt = jnp.minimum(pos + (c * CHUNK + 1), w).astype(F32)
            pooled_ref[c * CHUNK:(c + 1) * CHUNK, :] = (acc / cnt - tok).astype(BF16)
        mixed = _bdot(pooled_ref[...], wp_ref[g].astype(BF16)) * ps_ref[:, cols]
        o_ref[:, cols] = mixed.astype(o_ref.dtype)

    row_id = lax.broadcasted_iota(jnp.int32, (CHUNK, CHUNK), 0)
    col_id = lax.broadcasted_iota(jnp.int32, (CHUNK, CHUNK), 1)
    for h in range(N_GROUPS):
        cols = slice(h * gc, (h + 1) * gc)
        w_tri = jnp.where(row_id >= col_id, ws_ref[h], 0.0).astype(BF16)
        bias = bs_ref[h]
        for c in range(rows // CHUNK):
            rs = slice(c * CHUNK, (c + 1) * CHUNK)
            vr = v_ref[rs, cols]
            ms = jnp.mean(vr * vr, axis=-1, keepdims=True)
            vn = (vr * lax.rsqrt(ms + EPS) * gv_ref[:, cols]).astype(BF16)
            mixed = _bdot(w_tri, vn) + bias
            o_ref[rs, d_pool + h * gc:d_pool + (h + 1) * gc] = (u_ref[rs, cols] * mixed).astype(o_ref.dtype)


def _mix_prompt(pz, w_pool, pool_scale, g_v, w_s, b_s, *, batch, seq, rows):
    d_pool = pool_scale.shape[-1]
    gc = d_pool // N_GROUPS
    blocks_per_seq = seq // rows

    def row_blk(b, r):
        return b * blocks_per_seq + r

    def halo_blk(b, r):
        return jnp.maximum((b * seq + r * rows) // HALO - 1, 0)

    kern = functools.partial(_mix_prompt_kernel, rows=rows)
    return pl.pallas_call(
        kern,
        grid=(batch, blocks_per_seq),
        in_specs=[
            pl.BlockSpec((rows, d_pool), lambda b, r: (row_blk(b, r), 0)),
            pl.BlockSpec((HALO, d_pool), lambda b, r: (halo_blk(b, r), 0)),
            pl.BlockSpec((rows, d_pool), lambda b, r: (row_blk(b, r), 1)),
            pl.BlockSpec((rows, d_pool), lambda b, r: (row_blk(b, r), 2)),
            pl.BlockSpec((N_GROUPS, gc, gc), lambda b, r: (0, 0, 0)),
            pl.BlockSpec((1, d_pool), lambda b, r: (0, 0)),
            pl.BlockSpec((1, d_pool), lambda b, r: (0, 0)),
            pl.BlockSpec((N_GROUPS, CHUNK, CHUNK), lambda b, r: (0, 0, 0)),
            pl.BlockSpec((N_GROUPS, CHUNK, 1), lambda b, r: (0, 0, 0)),
        ],
        out_specs=pl.BlockSpec((rows, 2 * d_pool), lambda b, r: (row_blk(b, r), 0)),
        out_shape=jax.ShapeDtypeStruct((batch * seq, 2 * d_pool), BF16),
        scratch_shapes=[pltpu.VMEM((HALO + rows, d_pool), F32), pltpu.VMEM((rows, gc), BF16)],
        compiler_params=_params("arbitrary", "arbitrary"),
        name="mix_prompt",
    )(pz, pz, pz, pz, w_pool, pool_scale.reshape(1, d_pool), g_v.reshape(1, d_pool), w_s,
      b_s.reshape(N_GROUPS, CHUNK, 1))


def _mix_sample_kernel(ws_ref, bs_ref, p_ref, st_ref, u_ref, v_ref, wp_ref, ps_ref, gv_ref,
                       op_ref, og_ref, ov_ref, pooled_ref):
    g = pl.program_id(0)
    n_t, nb, _ = p_ref.shape
    n_state = st_ref.shape[0]

    for gg, w in enumerate(POOL_WINDOWS):
        @pl.when(g == gg)
        def _(w=w):
            for t in range(n_t):
                acc = p_ref[t]
                for back in range(1, w):
                    e = n_state + t - back
                    acc = acc + (p_ref[e - n_state] if e >= n_state else st_ref[e])
                cnt = float(min(PAST_LEN + t + 1, w))
                pooled_ref[t * nb:(t + 1) * nb, :] = (acc / cnt - p_ref[t]).astype(BF16)

    mixed = _bdot(pooled_ref[...], wp_ref[0].astype(BF16)) * ps_ref[...]
    for t in range(n_t):
        op_ref[t] = mixed[t * nb:(t + 1) * nb].astype(op_ref.dtype)

    vn = []
    for t in range(n_t):
        vr = v_ref[t]
        ms = jnp.mean(vr * vr, axis=-1, keepdims=True)
        vt = vr * lax.rsqrt(ms + EPS) * gv_ref[...]
        ov_ref[t] = vt
        vn.append(vt)
    for t in range(n_t):
        m = ws_ref[g, t * n_t] * vn[0]
        for s in range(1, t + 1):
            m = m + ws_ref[g, t * n_t + s] * vn[s]
        og_ref[t] = (u_ref[t] * (m + bs_ref[g, t])).astype(og_ref.dtype)


def _mix_sample(pz_tm, state_tm, w_pool, pool_scale, g_v, w_s, b_s):
    n_t, nb, _ = pz_tm.shape
    n_state = state_tm.shape[0]
    d_pool = pool_scale.shape[-1]
    gc = d_pool // N_GROUPS
    ws_small = w_s[:, :n_t, :n_t].reshape(N_GROUPS, n_t * n_t)
    bs_small = b_s[:, :n_t]
    smem = pl.BlockSpec(memory_space=pltpu.SMEM)
    slab = lambda off: pl.BlockSpec((n_t, nb, gc), lambda g: (0, 0, g + off))
    vec = pl.BlockSpec((1, gc), lambda g: (0, g))
    return pl.pallas_call(
        _mix_sample_kernel,
        grid=(N_GROUPS,),
        in_specs=[
            smem, smem,
            slab(0),
            pl.BlockSpec((n_state, nb, gc), lambda g: (0, 0, g)),
            slab(N_GROUPS), slab(2 * N_GROUPS),
            pl.BlockSpec((1, gc, gc), lambda g: (g, 0, 0)),
            vec, vec,
        ],
        out_specs=[pl.BlockSpec((n_t, nb, gc), lambda g: (0, 0, g)) for _ in range(3)],
        out_shape=[jax.ShapeDtypeStruct((n_t, nb, d_pool), BF16),
                   jax.ShapeDtypeStruct((n_t, nb, d_pool), BF16),
                   jax.ShapeDtypeStruct((n_t, nb, d_pool), F32)],
        scratch_shapes=[pltpu.VMEM((n_t * nb, gc), BF16)],
        compiler_params=_params("arbitrary"),
        name="mix_sample",
    )(ws_small, bs_small, pz_tm, state_tm, pz_tm, pz_tm, w_pool, pool_scale.reshape(1, d_pool),
      g_v.reshape(1, d_pool))


def _softmax_rows(s):
    m = jnp.max(s, axis=-1, keepdims=True)
    e = jnp.exp(s - m)
    return e / jnp.sum(e, axis=-1, keepdims=True)


def _attn_prompt_kernel(q_ref, k_ref, v_ref, o_ref, *, scale):
    s = lax.dot_general(q_ref[...], k_ref[...], (((1,), (1,)), ((), ())),
                        preferred_element_type=F32) * scale
    a = _softmax_rows(s).astype(BF16)
    o_ref[...] = _bdot(a, v_ref[...]).astype(o_ref.dtype)


def _attn_prompt(q, k, v, *, batch, seq, n_mem, tq):
    d = q.shape[1]
    hd = d // N_GROUPS
    qb = seq // tq
    kern = functools.partial(_attn_prompt_kernel, scale=hd ** -0.5)
    return pl.pallas_call(
        kern,
        grid=(batch, N_GROUPS, qb),
        in_specs=[
            pl.BlockSpec((tq, hd), lambda b, h, i: (b * qb + i, h)),
            pl.BlockSpec((n_mem, hd), lambda b, h, i: (b, h)),
            pl.BlockSpec((n_mem, hd), lambda b, h, i: (b, h)),
        ],
        out_specs=pl.BlockSpec((tq, hd), lambda b, h, i: (b * qb + i, h)),
        out_shape=jax.ShapeDtypeStruct((batch * seq, d), BF16),
        compiler_params=_params("arbitrary", "arbitrary", "arbitrary"),
        name="attn_prompt",
    )(q, k, v)


def _attn_sample_kernel(q_ref, k_ref, v_ref, o_ref, *, scale, n_q):
    _, nb, n_mem, n_heads, hd = k_ref.shape
    rows = n_q * n_heads
    r_head = lax.broadcasted_iota(jnp.int32, (rows, n_mem * n_heads), 0) % n_heads
    c_head = lax.broadcasted_iota(jnp.int32, (rows, n_mem * n_heads), 1) % n_heads
    same_head = r_head == c_head
    for b in range(nb):
        rs = slice(b * rows, (b + 1) * rows)
        qb = q_ref[rs, :].astype(BF16)
        km = k_ref[0, b].reshape(n_mem * n_heads, hd).astype(BF16)
        vm = v_ref[0, b].reshape(n_mem * n_heads, hd).astype(BF16)
        s = lax.dot_general(qb, km, (((1,), (1,)), ((), ())), preferred_element_type=F32) * scale
        a = _softmax_rows(jnp.where(same_head, s, MASKED_SCORE)).astype(BF16)
        o_ref[rs, :] = _bdot(a, vm)


def _attn_sample(q, mem_k, mem_v, *, layer, n_q, nb):
    _, batch, n_mem, n_heads, hd = mem_k.shape
    rows = n_q * n_heads
    kern = functools.partial(_attn_sample_kernel, scale=hd ** -0.5, n_q=n_q)
    mem_spec = pl.BlockSpec((1, nb, n_mem, n_heads, hd), lambda i: (layer, i, 0, 0, 0))
    return pl.pallas_call(
        kern,
        grid=(batch // nb,),
        in_specs=[pl.BlockSpec((nb * rows, hd), lambda i: (i, 0)), mem_spec, mem_spec],
        out_specs=pl.BlockSpec((nb * rows, hd), lambda i: (i, 0)),
        out_shape=jax.ShapeDtypeStruct((batch * rows, hd), F32),
        compiler_params=_params("arbitrary"),
        name="attn_sample",
    )(q, mem_k, mem_v)


def _layer(x, mixers, attention, lw, *, tm, tag):
    (g_mix, w_in, w_out, g_xattn, w_q, w_o, g_ffn, w_gate_up, w_down, d_pool) = lw
    pz = _norm_mm(x, g_mix, w_in, tm=tm, tn=512, out_dtypes=(F32,), gelu_from_col=d_pool,
                  name="in_proj_" + tag)[0]
    mix, extras = mixers(pz)
    x, xg, rs = _mm_res(mix, w_out, x, tm=tm, tn=512, a_buffers=2, g_next=g_xattn, name="out_proj_" + tag)
    q = _scaled_mm(xg, rs, w_q, tm=tm, tn=512, out_dtypes=attention.q_dtypes, name="q_proj_" + tag)[0]
    o = attention(q)
    x, xg, rs = _mm_res(o.astype(BF16), w_o, x, tm=tm, tn=512, a_buffers=2, g_next=g_ffn, name="o_proj_" + tag)
    act = _scaled_swiglu(xg, rs, w_gate_up, tm=tm, tn=256, name="ffn_up_" + tag)
    x = _mm_res(act, w_down, x, tm=tm, tn=256, a_buffers=1, name="ffn_down_" + tag)
    return x, pz, extras


def kernel(x_prompt, x_sample, cache_mem_k, cache_mem_v, state_pool, mem_prompt, g_mix, w_in, g_v, w_pool,
           pool_scale, w_s, b_s, w_out, g_xattn, g_mem, w_q, w_k, w_v, w_o, g_ffn, w_gate_up, w_down, g_final):
    depth = g_mix.shape[0]
    assert depth == 1, "single-layer step"
    batch, seq, d = x_prompt.shape
    dec_batch, dec_seq, _ = x_sample.shape
    n_mem = mem_prompt.shape[1]
    d_pool = pool_scale.shape[-1]
    l = 0
    lw = (g_mix[l], w_in[l], w_out[l], g_xattn[l], w_q[l], w_o[l], g_ffn[l], w_gate_up[l], w_down[l], d_pool)

    mem = mem_prompt.reshape(batch * n_mem, d)
    mk, mk_b = _norm_mm(mem, g_mem[l], w_k[l], tm=batch * n_mem, tn=512, out_dtypes=(F32, BF16), name="mem_k")
    mv, mv_b = _norm_mm(mem, g_mem[l], w_v[l], tm=batch * n_mem, tn=512, out_dtypes=(F32, BF16), name="mem_v")

    def prompt_mixers(pz):
        return _mix_prompt(pz, w_pool[l], pool_scale[l], g_v[l], w_s[l], b_s[l],
                           batch=batch, seq=seq, rows=512), None

    def prompt_attention(q):
        return _attn_prompt(q, mk_b, mv_b, batch=batch, seq=seq, n_mem=n_mem, tq=1024)
    prompt_attention.q_dtypes = (BF16,)

    xp, pz_p, _ = _layer(x_prompt.reshape(batch * seq, d), prompt_mixers, prompt_attention, lw,
                         tm=1024, tag="prompt")
    y_prompt = _rmsnorm(xp, g_final, tm=512, name="final_norm_prompt").reshape(batch, seq, d)
    n_state = state_pool.shape[2]
    pool_p = pz_p.reshape(batch, seq, -1)[:, seq - n_state:, :d_pool]

    def sample_mixers(pz):
        pz_tm = pz.reshape(dec_batch, dec_seq, -1).transpose(1, 0, 2)
        state_tm = state_pool[l].transpose(1, 0, 2)
        mp, mg, v_tm = _mix_sample(pz_tm, state_tm, w_pool[l], pool_scale[l], g_v[l], w_s[l], b_s[l])
        mix = jnp.concatenate([mp, mg], axis=-1).transpose(1, 0, 2).reshape(dec_batch * dec_seq, -1)
        return mix, v_tm.transpose(1, 0, 2)

    def sample_attention(q):
        n_heads, hd = cache_mem_k.shape[-2:]
        o = _attn_sample(q.reshape(-1, hd), cache_mem_k, cache_mem_v, layer=l, n_q=dec_seq, nb=2)
        return o.reshape(dec_batch * dec_seq, n_heads * hd)
    sample_attention.q_dtypes = (F32,)

    xs, pz_s, v_s = _layer(x_sample.reshape(dec_batch * dec_seq, d), sample_mixers, sample_attention, lw,
                           tm=dec_batch * dec_seq, tag="sample")
    y_sample = _rmsnorm(xs, g_final, tm=512, name="final_norm_sample").reshape(dec_batch, dec_seq, d)
    p_s = pz_s[:, :d_pool].reshape(dec_batch, dec_seq, d_pool)
    pool_s = jnp.concatenate([state_pool[l], p_s], axis=1)[:, -n_state:]

    hd = d // N_GROUPS
    return (y_prompt, y_sample,
            mk.reshape(1, batch, n_mem, N_GROUPS, hd), mv.reshape(1, batch, n_mem, N_GROUPS, hd),
            pool_p[None], pool_s[None], v_s[None])
```

```python
import functools
import math

import jax
import jax.numpy as jnp
from jax import lax
from jax.experimental import pallas as pl
from jax.experimental.pallas import tpu as pltpu

EPS = 1e-6
POOL_WINDOWS = (2, 4, 8, 16)
N_GROUPS = 4
CHUNK = 128
HALO = 16
PAST_LEN = 16384
MASKED_SCORE = -1e30

V7X_VMEM_LIMIT_BYTES = 60000 * 1024

BF16 = jnp.bfloat16
F32 = jnp.float32


def _params(*sem):
    return pltpu.CompilerParams(dimension_semantics=sem, vmem_limit_bytes=V7X_VMEM_LIMIT_BYTES)


def _bdot(a, b):
    return jnp.dot(a, b, preferred_element_type=F32)


def _norm_rows_into(x_ref, g_ref, h_ref, rows=64):
    def body(c, carry):
        r0 = pl.multiple_of(c * rows, rows)
        x = x_ref[pl.ds(r0, rows), :]
        ms = jnp.mean(x * x, axis=-1, keepdims=True)
        h_ref[pl.ds(r0, rows), :] = (x * lax.rsqrt(ms + EPS) * g_ref[...]).astype(h_ref.dtype)
        return carry
    lax.fori_loop(0, x_ref.shape[0] // rows, body, 0)


def _gelu(x):
    return 0.5 * x * (1.0 + lax.erf(x * math.sqrt(0.5)))


def _norm_mm_kernel(x_ref, g_ref, w_ref, *refs, n_out, gelu_from):
    outs, h_ref = refs[:n_out], refs[n_out]
    j = pl.program_id(1)

    @pl.when(j == 0)
    def _():
        _norm_rows_into(x_ref, g_ref, h_ref)

    acc = _bdot(h_ref[...], w_ref[...].astype(BF16))
    if gelu_from is None:
        for o in outs:
            o[...] = acc.astype(o.dtype)
    else:
        @pl.when(j < gelu_from)
        def _():
            outs[0][...] = acc

        @pl.when(j >= gelu_from)
        def _():
            outs[0][...] = _gelu(acc)


def _norm_mm(x, g, w, *, tm, tn, out_dtypes, gelu_from_col=None, name):
    m, k = x.shape
    n = w.shape[1]
    kern = functools.partial(_norm_mm_kernel, n_out=len(out_dtypes),
                             gelu_from=None if gelu_from_col is None else gelu_from_col // tn)
    return pl.pallas_call(
        kern,
        grid=(m // tm, n // tn),
        in_specs=[
            pl.BlockSpec((tm, k), lambda i, j: (i, 0), pipeline_mode=pl.Buffered(1)),
            pl.BlockSpec((1, k), lambda i, j: (0, 0)),
            pl.BlockSpec((k, tn), lambda i, j: (0, j)),
        ],
        out_specs=[pl.BlockSpec((tm, tn), lambda i, j: (i, j)) for _ in out_dtypes],
        out_shape=[jax.ShapeDtypeStruct((m, n), d) for d in out_dtypes],
        scratch_shapes=[pltpu.VMEM((tm, k), BF16)],
        compiler_params=_params("arbitrary", "arbitrary"),
        name=name,
    )(x, g.reshape(1, k), w)


def _scaled_mm_kernel(a_ref, rs_ref, w_ref, *outs):
    acc = _bdot(a_ref[...], w_ref[...].astype(BF16)) * rs_ref[...]
    for o in outs:
        o[...] = acc.astype(o.dtype)


def _scaled_mm(xg, rs, w, *, tm, tn, out_dtypes, name):
    m, k = xg.shape
    n = w.shape[1]
    return pl.pallas_call(
        _scaled_mm_kernel,
        grid=(m // tm, n // tn),
        in_specs=[
            pl.BlockSpec((tm, k), lambda i, j: (i, 0)),
            pl.BlockSpec((tm, 1), lambda i, j: (i, 0)),
            pl.BlockSpec((k, tn), lambda i, j: (0, j)),
        ],
        out_specs=[pl.BlockSpec((tm, tn), lambda i, j: (i, j)) for _ in out_dtypes],
        out_shape=[jax.ShapeDtypeStruct((m, n), d) for d in out_dtypes],
        compiler_params=_params("arbitrary", "arbitrary"),
        name=name,
    )(xg, rs, w)


def _scaled_swiglu_kernel(a_ref, rs_ref, wg_ref, wu_ref, o_ref):
    a = a_ref[...]
    rs = rs_ref[...]
    gate = _bdot(a, wg_ref[...].astype(BF16)) * rs
    up = _bdot(a, wu_ref[...].astype(BF16)) * rs
    o_ref[...] = (gate * (1.0 / (1.0 + jnp.exp(-gate))) * up).astype(o_ref.dtype)


def _scaled_swiglu(xg, rs, w_gate_up, *, tm, tn, name):
    m, k = xg.shape
    d_ff = w_gate_up.shape[1] // 2
    nt = d_ff // tn
    return pl.pallas_call(
        _scaled_swiglu_kernel,
        grid=(m // tm, nt),
        in_specs=[
            pl.BlockSpec((tm, k), lambda i, j: (i, 0)),
            pl.BlockSpec((tm, 1), lambda i, j: (i, 0)),
            pl.BlockSpec((k, tn), lambda i, j: (0, j)),
            pl.BlockSpec((k, tn), lambda i, j: (0, j + nt)),
        ],
        out_specs=pl.BlockSpec((tm, tn), lambda i, j: (i, j)),
        out_shape=jax.ShapeDtypeStruct((m, d_ff), BF16),
        compiler_params=_params("arbitrary", "arbitrary"),
        name=name,
    )(xg, rs, w_gate_up, w_gate_up)


def _mm_res_kernel(a_ref, w_ref, r_ref, *rest, n_cols):
    x = r_ref[...] + _bdot(a_ref[...], w_ref[...].astype(BF16))
    if len(rest) == 1:
        rest[0][...] = x
        return
    g_ref, o_ref, xg_ref, rs_ref, ssq_ref = rest
    j = pl.program_id(1)
    o_ref[...] = x
    xg_ref[...] = (x * g_ref[...]).astype(xg_ref.dtype)
    part = jnp.sum(x * x, axis=-1, keepdims=True)

    @pl.when(j == 0)
    def _():
        ssq_ref[...] = part

    @pl.when(j > 0)
    def _():
        ssq_ref[...] += part

    @pl.when(j == pl.num_programs(1) - 1)
    def _():
        rs_ref[...] = lax.rsqrt(ssq_ref[...] / n_cols + EPS)


def _mm_res(a, w, res, *, tm, tn, a_buffers, name, g_next=None):
    m, k = a.shape
    n = w.shape[1]
    tile = pl.BlockSpec((tm, tn), lambda i, j: (i, j))
    in_specs = [
        pl.BlockSpec((tm, k), lambda i, j: (i, 0), pipeline_mode=pl.Buffered(a_buffers)),
        pl.BlockSpec((k, tn), lambda i, j: (0, j)),
        tile,
    ]
    args = [a, w, res]
    out_specs, out_shape, scratch = tile, jax.ShapeDtypeStruct((m, n), F32), []
    if g_next is not None:
        in_specs.append(pl.BlockSpec((1, tn), lambda i, j: (0, j)))
        args.append(g_next.reshape(1, n))
        out_specs = [tile, tile, pl.BlockSpec((tm, 1), lambda i, j: (i, 0))]
        out_shape = [out_shape, jax.ShapeDtypeStruct((m, n), BF16), jax.ShapeDtypeStruct((m, 1), F32)]
        scratch = [pltpu.VMEM((tm, 1), F32)]
    return pl.pallas_call(
        functools.partial(_mm_res_kernel, n_cols=n),
        grid=(m // tm, n // tn),
        in_specs=in_specs,
        out_specs=out_specs,
        out_shape=out_shape,
        scratch_shapes=scratch,
        compiler_params=_params("arbitrary", "arbitrary"),
        name=name,
    )(*args)


def _rmsnorm_kernel(x_ref, g_ref, o_ref):
    x = x_ref[...]
    ms = jnp.mean(x * x, axis=-1, keepdims=True)
    o_ref[...] = x * lax.rsqrt(ms + EPS) * g_ref[...]


def _rmsnorm(x, g, *, tm, name):
    m, k = x.shape
    return pl.pallas_call(
        _rmsnorm_kernel,
        grid=(m // tm,),
        in_specs=[pl.BlockSpec((tm, k), lambda i: (i, 0)), pl.BlockSpec((1, k), lambda i: (0, 0))],
        out_specs=pl.BlockSpec((tm, k), lambda i: (i, 0)),
        out_shape=jax.ShapeDtypeStruct((m, k), F32),
        compiler_params=_params("arbitrary"),
        name=name,
    )(x, g.reshape(1, k))


def _mix_prompt_kernel(p_ref, halo_ref, u_ref, v_ref, wp_ref, ps_ref, gv_ref, ws_ref, bs_ref,
                       o_ref, ext_ref, pooled_ref, *, rows):
    r = pl.program_id(1)
    d_pool = p_ref.shape[1]
    gc = d_pool // N_GROUPS

    @pl.when(r == 0)
    def _():
        ext_ref[0:HALO, :] = jnp.zeros((HALO, d_pool), F32)

    @pl.when(r > 0)
    def _():
        ext_ref[0:HALO, :] = halo_ref[...]

    ext_ref[HALO:HALO + rows, :] = p_ref[...]

    pos = r * rows + lax.broadcasted_iota(jnp.int32, (CHUNK, 1), 0)
    for g, w in enumerate(POOL_WINDOWS):
        cols = slice(g * gc, (g + 1) * gc)
        for c in range(rows // CHUNK):
            base = HALO + c * CHUNK
            tok = ext_ref[base:base + CHUNK, cols]
            acc = tok
            for back in range(1, w):
                acc = acc + ext_ref[base - back:base - back + CHUNK, cols]
            cnt = jnp.minimum(pos + (c * CHUNK + 1), w).astype(F32)
            pooled_ref[c * CHUNK:(c + 1) * CHUNK, :] = (acc / cnt - tok).astype(BF16)
        mixed = _bdot(pooled_ref[...], wp_ref[g].astype(BF16)) * ps_ref[:, cols]
        o_ref[:, cols] = mixed.astype(o_ref.dtype)

    row_id = lax.broadcasted_iota(jnp.int32, (CHUNK, CHUNK), 0)
    col_id = lax.broadcasted_iota(jnp.int32, (CHUNK, CHUNK), 1)
    for h in range(N_GROUPS):
        cols = slice(h * gc, (h + 1) * gc)
        w_tri = jnp.where(row_id >= col_id, ws_ref[h], 0.0).astype(BF16)
        bias = bs_ref[h]
        for c in range(rows // CHUNK):
            rs = slice(c * CHUNK, (c + 1) * CHUNK)
            vr = v_ref[rs, cols]
            ms = jnp.mean(vr * vr, axis=-1, keepdims=True)
            vn = (vr * lax.rsqrt(ms + EPS) * gv_ref[:, cols]).astype(BF16)
            mixed = _bdot(w_tri, vn) + bias
            o_ref[rs, d_pool + h * gc:d_pool + (h + 1) * gc] = (u_ref[rs, cols] * mixed).astype(o_ref.dtype)


def _mix_prompt(pz, w_pool, pool_scale, g_v, w_s, b_s, *, batch, seq, rows):
    d_pool = pool_scale.shape[-1]
    gc = d_pool // N_GROUPS
    blocks_per_seq = seq // rows

    def row_blk(b, r):
        return b * blocks_per_seq + r

    def halo_blk(b, r):
        return jnp.maximum((b * seq + r * rows) // HALO - 1, 0)

    kern = functools.partial(_mix_prompt_kernel, rows=rows)
    return pl.pallas_call(
        kern,
        grid=(batch, blocks_per_seq),
        in_specs=[
            pl.BlockSpec((rows, d_pool), lambda b, r: (row_blk(b, r), 0)),
            pl.BlockSpec((HALO, d_pool), lambda b, r: (halo_blk(b, r), 0)),
            pl.BlockSpec((rows, d_pool), lambda b, r: (row_blk(b, r), 1)),
            pl.BlockSpec((rows, d_pool), lambda b, r: (row_blk(b, r), 2)),
            pl.BlockSpec((N_GROUPS, gc, gc), lambda b, r: (0, 0, 0)),
            pl.BlockSpec((1, d_pool), lambda b, r: (0, 0)),
            pl.BlockSpec((1, d_pool), lambda b, r: (0, 0)),
            pl.BlockSpec((N_GROUPS, CHUNK, CHUNK), lambda b, r: (0, 0, 0)),
            pl.BlockSpec((N_GROUPS, CHUNK, 1), lambda b, r: (0, 0, 0)),
        ],
        out_specs=pl.BlockSpec((rows, 2 * d_pool), lambda b, r: (row_blk(b, r), 0)),
        out_shape=jax.ShapeDtypeStruct((batch * seq, 2 * d_pool), BF16),
        scratch_shapes=[pltpu.VMEM((HALO + rows, d_pool), F32), pltpu.VMEM((rows, gc), BF16)],
        compiler_params=_params("arbitrary", "arbitrary"),
        name="mix_prompt",
    )(pz, pz, pz, pz, w_pool, pool_scale.reshape(1, d_pool), g_v.reshape(1, d_pool), w_s,
      b_s.reshape(N_GROUPS, CHUNK, 1))


def _mix_sample_kernel(ws_ref, bs_ref, p_ref, st_ref, u_ref, v_ref, wp_ref, ps_ref, gv_ref,
                       op_ref, og_ref, ov_ref, pooled_ref):
    g = pl.program_id(0)
    n_t, nb, _ = p_ref.shape
    n_state = st_ref.shape[0]

    for gg, w in enumerate(POOL_WINDOWS):
        @pl.when(g == gg)
        def _(w=w):
            for t in range(n_t):
                acc = p_ref[t]
                for back in range(1, w):
                    e = n_state + t - back
                    acc = acc + (p_ref[e - n_state] if e >= n_state else st_ref[e])
                cnt = float(min(PAST_LEN + t + 1, w))
                pooled_ref[t * nb:(t + 1) * nb, :] = (acc / cnt - p_ref[t]).astype(BF16)

    mixed = _bdot(pooled_ref[...], wp_ref[0].astype(BF16)) * ps_ref[...]
    for t in range(n_t):
        op_ref[t] = mixed[t * nb:(t + 1) * nb].astype(op_ref.dtype)

    vn = []
    for t in range(n_t):
        vr = v_ref[t]
        ms = jnp.mean(vr * vr, axis=-1, keepdims=True)
        vt = vr * lax.rsqrt(ms + EPS) * gv_ref[...]
        ov_ref[t] = vt
        vn.append(vt)
    for t in range(n_t):
        m = ws_ref[g, t * n_t] * vn[0]
        for s in range(1, t + 1):
            m = m + ws_ref[g, t * n_t + s] * vn[s]
        og_ref[t] = (u_ref[t] * (m + bs_ref[g, t])).astype(og_ref.dtype)


def _mix_sample(pz_tm, state_tm, w_pool, pool_scale, g_v, w_s, b_s):
    n_t, nb, _ = pz_tm.shape
    n_state = state_tm.shape[0]
    d_pool = pool_scale.shape[-1]
    gc = d_pool // N_GROUPS
    ws_small = w_s[:, :n_t, :n_t].reshape(N_GROUPS, n_t * n_t)
    bs_small = b_s[:, :n_t]
    smem = pl.BlockSpec(memory_space=pltpu.SMEM)
    slab = lambda off: pl.BlockSpec((n_t, nb, gc), lambda g: (0, 0, g + off))
    vec = pl.BlockSpec((1, gc), lambda g: (0, g))
    return pl.pallas_call(
        _mix_sample_kernel,
        grid=(N_GROUPS,),
        in_specs=[
            smem, smem,
            slab(0),
            pl.BlockSpec((n_state, nb, gc), lambda g: (0, 0, g)),
            slab(N_GROUPS), slab(2 * N_GROUPS),
            pl.BlockSpec((1, gc, gc), lambda g: (g, 0, 0)),
            vec, vec,
        ],
        out_specs=[pl.BlockSpec((n_t, nb, gc), lambda g: (0, 0, g)) for _ in range(3)],
        out_shape=[jax.ShapeDtypeStruct((n_t, nb, d_pool), BF16),
                   jax.ShapeDtypeStruct((n_t, nb, d_pool), BF16),
                   jax.ShapeDtypeStruct((n_t, nb, d_pool), F32)],
        scratch_shapes=[pltpu.VMEM((n_t * nb, gc), BF16)],
        compiler_params=_params("arbitrary"),
        name="mix_sample",
    )(ws_small, bs_small, pz_tm, state_tm, pz_tm, pz_tm, w_pool, pool_scale.reshape(1, d_pool),
      g_v.reshape(1, d_pool))


def _softmax_rows(s):
    m = jnp.max(s, axis=-1, keepdims=True)
    e = jnp.exp(s - m)
    return e / jnp.sum(e, axis=-1, keepdims=True)


def _scores_t(q, k):
    return lax.dot_general(q, k, (((1,), (1,)), ((), ())), preferred_element_type=F32)


def _attn_kernel(qs_ref, ks_ref, vs_ref, qp_ref, kp_ref, vp_ref, os_ref, op_ref, *, scale, n_q):
    _, nb, n_mem, n_heads, hd = ks_ref.shape
    rows = n_q * n_heads
    r_head = lax.broadcasted_iota(jnp.int32, (rows, n_mem * n_heads), 0) % n_heads
    c_head = lax.broadcasted_iota(jnp.int32, (rows, n_mem * n_heads), 1) % n_heads
    same_head = r_head == c_head
    for b in range(nb):
        rs = slice(b * rows, (b + 1) * rows)
        km = ks_ref[0, b].reshape(n_mem * n_heads, hd).astype(BF16)
        vm = vs_ref[0, b].reshape(n_mem * n_heads, hd).astype(BF16)
        s = _scores_t(qs_ref[rs, :].astype(BF16), km) * scale
        a = _softmax_rows(jnp.where(same_head, s, MASKED_SCORE)).astype(BF16)
        os_ref[rs, :] = _bdot(a, vm)

    a = _softmax_rows(_scores_t(qp_ref[...], kp_ref[...]) * scale).astype(BF16)
    op_ref[...] = _bdot(a, vp_ref[...]).astype(op_ref.dtype)


def _attention(q_s, mem_k_s, mem_v_s, q_p, k_p, v_p, *, layer, n_q, nb, batch, seq):
    _, dec_batch, n_mem, n_heads, hd = mem_k_s.shape
    rows = n_q * n_heads
    n_steps = dec_batch // nb
    tq = batch * n_heads * seq // n_steps
    assert tq * n_steps == batch * n_heads * seq and seq % tq == 0 and tq % 16 == 0
    per_pair = seq // tq

    def p_rows(i):
        pair = i // per_pair
        return (pair // n_heads) * per_pair + i % per_pair, pair % n_heads

    def p_mem(i):
        pair = i // per_pair
        return pair // n_heads, pair % n_heads

    mem_spec = pl.BlockSpec((1, nb, n_mem, n_heads, hd), lambda i: (layer, i, 0, 0, 0))
    return pl.pallas_call(
        functools.partial(_attn_kernel, scale=hd ** -0.5, n_q=n_q),
        grid=(n_steps,),
        in_specs=[
            pl.BlockSpec((nb * rows, hd), lambda i: (i, 0)), mem_spec, mem_spec,
            pl.BlockSpec((tq, hd), p_rows), pl.BlockSpec((n_mem, hd), p_mem), pl.BlockSpec((n_mem, hd), p_mem),
        ],
        out_specs=[pl.BlockSpec((nb * rows, hd), lambda i: (i, 0)), pl.BlockSpec((tq, hd), p_rows)],
        out_shape=[jax.ShapeDtypeStruct((dec_batch * rows, hd), F32),
                   jax.ShapeDtypeStruct((batch * seq, n_heads * hd), BF16)],
        compiler_params=_params("arbitrary"),
        name="attention",
    )(q_s, mem_k_s, mem_v_s, q_p, k_p, v_p)


def _layer_front(x, mixers, lw, *, tm, q_dtype, tag):
    (g_mix, w_in, w_out, g_xattn, w_q, w_o, g_ffn, w_gate_up, w_down, d_pool) = lw
    pz = _norm_mm(x, g_mix, w_in, tm=tm, tn=512, out_dtypes=(F32,), gelu_from_col=d_pool,
                  name="in_proj_" + tag)[0]
    mix, extras = mixers(pz)
    x, xg, rs = _mm_res(mix, w_out, x, tm=tm, tn=512, a_buffers=2, g_next=g_xattn, name="out_proj_" + tag)
    q = _scaled_mm(xg, rs, w_q, tm=tm, tn=512, out_dtypes=(q_dtype,), name="q_proj_" + tag)[0]
    return x, q, pz, extras


def _layer_back(x, o, lw, *, tm, tm_up, tag):
    (g_mix, w_in, w_out, g_xattn, w_q, w_o, g_ffn, w_gate_up, w_down, d_pool) = lw
    x, xg, rs = _mm_res(o, w_o, x, tm=tm, tn=512, a_buffers=2, g_next=g_ffn, name="o_proj_" + tag)
    act = _scaled_swiglu(xg, rs, w_gate_up, tm=tm_up, tn=256, name="ffn_up_" + tag)
    return _mm_res(act, w_down, x, tm=tm, tn=256, a_buffers=1, name="ffn_down_" + tag)


def kernel(x_prompt, x_sample, cache_mem_k, cache_mem_v, state_pool, mem_prompt, g_mix, w_in, g_v, w_pool,
           pool_scale, w_s, b_s, w_out, g_xattn, g_mem, w_q, w_k, w_v, w_o, g_ffn, w_gate_up, w_down, g_final):
    depth = g_mix.shape[0]
    assert depth == 1, "single-layer step"
    batch, seq, d = x_prompt.shape
    dec_batch, dec_seq, _ = x_sample.shape
    n_mem = mem_prompt.shape[1]
    n_heads, hd = cache_mem_k.shape[-2:]
    d_pool = pool_scale.shape[-1]
    n_state = state_pool.shape[2]
    n_sample = dec_batch * dec_seq
    l = 0
    lw = (g_mix[l], w_in[l], w_out[l], g_xattn[l], w_q[l], w_o[l], g_ffn[l], w_gate_up[l], w_down[l], d_pool)

    mem = mem_prompt.reshape(batch * n_mem, d)
    mk, mk_b = _norm_mm(mem, g_mem[l], w_k[l], tm=batch * n_mem, tn=512, out_dtypes=(F32, BF16), name="mem_k")
    mv, mv_b = _norm_mm(mem, g_mem[l], w_v[l], tm=batch * n_mem, tn=512, out_dtypes=(F32, BF16), name="mem_v")

    def prompt_mixers(pz):
        return _mix_prompt(pz, w_pool[l], pool_scale[l], g_v[l], w_s[l], b_s[l],
                           batch=batch, seq=seq, rows=512), None

    def sample_mixers(pz):
        pz_tm = pz.reshape(dec_batch, dec_seq, -1).transpose(1, 0, 2)
        state_tm = state_pool[l].transpose(1, 0, 2)
        mp, mg, v_tm = _mix_sample(pz_tm, state_tm, w_pool[l], pool_scale[l], g_v[l], w_s[l], b_s[l])
        mix = jnp.concatenate([mp, mg], axis=-1).transpose(1, 0, 2).reshape(n_sample, -1)
        return mix, v_tm.transpose(1, 0, 2)

    xp, q_p, pz_p, _ = _layer_front(x_prompt.reshape(batch * seq, d), prompt_mixers, lw,
                                    tm=1024, q_dtype=BF16, tag="prompt")
    xs, q_s, pz_s, v_s = _layer_front(x_sample.reshape(n_sample, d), sample_mixers, lw,
                                      tm=n_sample, q_dtype=F32, tag="sample")

    o_s, o_p = _attention(q_s.reshape(n_sample * n_heads, hd), cache_mem_k, cache_mem_v, q_p, mk_b, mv_b,
                          layer=l, n_q=dec_seq, nb=2, batch=batch, seq=seq)
    o_s = o_s.reshape(n_sample, d).astype(BF16)

    xp = _layer_back(xp, o_p, lw, tm=1024, tm_up=2048, tag="prompt")
    xs = _layer_back(xs, o_s, lw, tm=n_sample, tm_up=n_sample, tag="sample")

    y_prompt = _rmsnorm(xp, g_final, tm=512, name="final_norm_prompt").reshape(batch, seq, d)
    y_sample = _rmsnorm(xs, g_final, tm=512, name="final_norm_sample").reshape(dec_batch, dec_seq, d)
    pool_p = pz_p.reshape(batch, seq, -1)[:, seq - n_state:, :d_pool]
    p_s = pz_s[:, :d_pool].reshape(dec_batch, dec_seq, d_pool)
    pool_s = jnp.concatenate([state_pool[l], p_s], axis=1)[:, -n_state:]

    return (y_prompt, y_sample,
            mk.reshape(1, batch, n_mem, n_heads, hd), mv.reshape(1, batch, n_mem, n_heads, hd),
            pool_p[None], pool_s[None], v_s[None])
```

```python
import functools
import math

import jax
import jax.numpy as jnp
from jax import lax
from jax.experimental import pallas as pl
from jax.experimental.pallas import tpu as pltpu

EPS = 1e-6
POOL_WINDOWS = (2, 4, 8, 16)
N_GROUPS = 4
CHUNK = 128
HALO = 16
PAST_LEN = 16384
MASKED_SCORE = -1e30

V7X_VMEM_LIMIT_BYTES = 60000 * 1024

BF16 = jnp.bfloat16
F32 = jnp.float32


def _params(*sem):
    return pltpu.CompilerParams(dimension_semantics=sem, vmem_limit_bytes=V7X_VMEM_LIMIT_BYTES)


def _bdot(a, b):
    return jnp.dot(a, b, preferred_element_type=F32)


def _norm_rows_into(x_ref, g_ref, h_ref, rows=64):
    def body(c, carry):
        r0 = pl.multiple_of(c * rows, rows)
        x = x_ref[pl.ds(r0, rows), :]
        ms = jnp.mean(x * x, axis=-1, keepdims=True)
        h_ref[pl.ds(r0, rows), :] = (x * lax.rsqrt(ms + EPS) * g_ref[...]).astype(h_ref.dtype)
        return carry
    lax.fori_loop(0, x_ref.shape[0] // rows, body, 0)


def _gelu(x):
    return 0.5 * x * (1.0 + lax.erf(x * math.sqrt(0.5)))


def _norm_mm_kernel(x_ref, g_ref, w_ref, *refs, n_out, gelu_from):
    outs, h_ref = refs[:n_out], refs[n_out]
    j = pl.program_id(1)

    @pl.when(j == 0)
    def _():
        _norm_rows_into(x_ref, g_ref, h_ref)

    acc = _bdot(h_ref[...], w_ref[...].astype(BF16))
    if gelu_from is None:
        for o in outs:
            o[...] = acc.astype(o.dtype)
    else:
        @pl.when(j < gelu_from)
        def _():
            outs[0][...] = acc

        @pl.when(j >= gelu_from)
        def _():
            outs[0][...] = _gelu(acc)


def _norm_mm(x, g, w, *, tm, tn, out_dtypes, gelu_from_col=None, name):
    m, k = x.shape
    n = w.shape[1]
    kern = functools.partial(_norm_mm_kernel, n_out=len(out_dtypes),
                             gelu_from=None if gelu_from_col is None else gelu_from_col // tn)
    return pl.pallas_call(
        kern,
        grid=(m // tm, n // tn),
        in_specs=[
            pl.BlockSpec((tm, k), lambda i, j: (i, 0), pipeline_mode=pl.Buffered(1)),
            pl.BlockSpec((1, k), lambda i, j: (0, 0)),
            pl.BlockSpec((k, tn), lambda i, j: (0, j)),
        ],
        out_specs=[pl.BlockSpec((tm, tn), lambda i, j: (i, j)) for _ in out_dtypes],
        out_shape=[jax.ShapeDtypeStruct((m, n), d) for d in out_dtypes],
        scratch_shapes=[pltpu.VMEM((tm, k), BF16)],
        compiler_params=_params("arbitrary", "arbitrary"),
        name=name,
    )(x, g.reshape(1, k), w)


def _scaled_mm_kernel(a_ref, rs_ref, w_ref, *outs):
    acc = _bdot(a_ref[...], w_ref[...].astype(BF16)) * rs_ref[...]
    for o in outs:
        o[...] = acc.astype(o.dtype)


def _scaled_mm(xg, rs, w, *, tm, tn, out_dtypes, name):
    m, k = xg.shape
    n = w.shape[1]
    return pl.pallas_call(
        _scaled_mm_kernel,
        grid=(m // tm, n // tn),
        in_specs=[
            pl.BlockSpec((tm, k), lambda i, j: (i, 0)),
            pl.BlockSpec((tm, 1), lambda i, j: (i, 0)),
            pl.BlockSpec((k, tn), lambda i, j: (0, j)),
        ],
        out_specs=[pl.BlockSpec((tm, tn), lambda i, j: (i, j)) for _ in out_dtypes],
        out_shape=[jax.ShapeDtypeStruct((m, n), d) for d in out_dtypes],
        compiler_params=_params("arbitrary", "arbitrary"),
        name=name,
    )(xg, rs, w)


def _scaled_swiglu_kernel(a_ref, rs_ref, wg_ref, wu_ref, o_ref):
    a = a_ref[...]
    rs = rs_ref[...]
    gate = _bdot(a, wg_ref[...].astype(BF16)) * rs
    up = _bdot(a, wu_ref[...].astype(BF16)) * rs
    o_ref[...] = (gate * (1.0 / (1.0 + jnp.exp(-gate))) * up).astype(o_ref.dtype)


def _scaled_swiglu(xg, rs, w_gate_up, *, tm, tn, name):
    m, k = xg.shape
    d_ff = w_gate_up.shape[1] // 2
    nt = d_ff // tn
    return pl.pallas_call(
        _scaled_swiglu_kernel,
        grid=(m // tm, nt),
        in_specs=[
            pl.BlockSpec((tm, k), lambda i, j: (i, 0)),
            pl.BlockSpec((tm, 1), lambda i, j: (i, 0)),
            pl.BlockSpec((k, tn), lambda i, j: (0, j)),
            pl.BlockSpec((k, tn), lambda i, j: (0, j + nt)),
        ],
        out_specs=pl.BlockSpec((tm, tn), lambda i, j: (i, j)),
        out_shape=jax.ShapeDtypeStruct((m, d_ff), BF16),
        compiler_params=_params("arbitrary", "arbitrary"),
        name=name,
    )(xg, rs, w_gate_up, w_gate_up)


def _mm_res_kernel(a_ref, w_ref, r_ref, *rest, n_cols):
    x = r_ref[...] + _bdot(a_ref[...].astype(BF16), w_ref[...].astype(BF16))
    if len(rest) == 1:
        rest[0][...] = x
        return
    g_ref, o_ref, xg_ref, rs_ref, ssq_ref = rest
    j = pl.program_id(1)
    o_ref[...] = x
    xg_ref[...] = (x * g_ref[...]).astype(xg_ref.dtype)
    part = jnp.sum(x * x, axis=-1, keepdims=True)

    @pl.when(j == 0)
    def _():
        ssq_ref[...] = part

    @pl.when(j > 0)
    def _():
        ssq_ref[...] += part

    @pl.when(j == pl.num_programs(1) - 1)
    def _():
        rs_ref[...] = lax.rsqrt(ssq_ref[...] / n_cols + EPS)


def _mm_res(a, w, res, *, tm, tn, a_buffers, name, g_next=None):
    m, k = a.shape
    n = w.shape[1]
    tile = pl.BlockSpec((tm, tn), lambda i, j: (i, j))
    in_specs = [
        pl.BlockSpec((tm, k), lambda i, j: (i, 0), pipeline_mode=pl.Buffered(a_buffers)),
        pl.BlockSpec((k, tn), lambda i, j: (0, j)),
        tile,
    ]
    args = [a, w, res]
    out_specs, out_shape, scratch = tile, jax.ShapeDtypeStruct((m, n), F32), []
    if g_next is not None:
        in_specs.append(pl.BlockSpec((1, tn), lambda i, j: (0, j)))
        args.append(g_next.reshape(1, n))
        out_specs = [tile, tile, pl.BlockSpec((tm, 1), lambda i, j: (i, 0))]
        out_shape = [out_shape, jax.ShapeDtypeStruct((m, n), BF16), jax.ShapeDtypeStruct((m, 1), F32)]
        scratch = [pltpu.VMEM((tm, 1), F32)]
    return pl.pallas_call(
        functools.partial(_mm_res_kernel, n_cols=n),
        grid=(m // tm, n // tn),
        in_specs=in_specs,
        out_specs=out_specs,
        out_shape=out_shape,
        scratch_shapes=scratch,
        compiler_params=_params("arbitrary", "arbitrary"),
        name=name,
    )(*args)


def _rmsnorm_kernel(x_ref, g_ref, o_ref):
    x = x_ref[...]
    ms = jnp.mean(x * x, axis=-1, keepdims=True)
    o_ref[...] = (x * lax.rsqrt(ms + EPS) * g_ref[...]).reshape(o_ref.shape)


def _rmsnorm(x, g, *, tm, name, seq_len=None):
    m, k = x.shape
    if seq_len is None:
        out_spec, out_shape = pl.BlockSpec((tm, k), lambda i: (i, 0)), (m, k)
    else:
        out_spec = pl.BlockSpec((tm // seq_len, seq_len, k), lambda i: (i, 0, 0))
        out_shape = (m // seq_len, seq_len, k)
    return pl.pallas_call(
        _rmsnorm_kernel,
        grid=(m // tm,),
        in_specs=[pl.BlockSpec((tm, k), lambda i: (i, 0)), pl.BlockSpec((1, k), lambda i: (0, 0))],
        out_specs=out_spec,
        out_shape=jax.ShapeDtypeStruct(out_shape, F32),
        compiler_params=_params("arbitrary"),
        name=name,
    )(x, g.reshape(1, k))


def _mix_prompt_kernel(p_ref, halo_ref, u_ref, v_ref, wp_ref, ps_ref, gv_ref, ws_ref, bs_ref,
                       o_ref, ext_ref, pooled_ref, *, rows):
    r = pl.program_id(1)
    d_pool = p_ref.shape[1]
    gc = d_pool // N_GROUPS

    @pl.when(r == 0)
    def _():
        ext_ref[0:HALO, :] = jnp.zeros((HALO, d_pool), F32)

    @pl.when(r > 0)
    def _():
        ext_ref[0:HALO, :] = halo_ref[...]

    ext_ref[HALO:HALO + rows, :] = p_ref[...]

    pos = r * rows + lax.broadcasted_iota(jnp.int32, (CHUNK, 1), 0)
    for g, w in enumerate(POOL_WINDOWS):
        cols = slice(g * gc, (g + 1) * gc)
        for c in range(rows // CHUNK):
            base = HALO + c * CHUNK
            tok = ext_ref[base:base + CHUNK, cols]
            acc = tok
            for back in range(1, w):
                acc = acc + ext_ref[base - back:base - back + CHUNK, cols]
            cnt = jnp.minimum(pos + (c * CHUNK + 1), w).astype(F32)
            pooled_ref[c * CHUNK:(c + 1) * CHUNK, :] = (acc / cnt - tok).astype(BF16)
        mixed = _bdot(pooled_ref[...], wp_ref[g].astype(BF16)) * ps_ref[:, cols]
        o_ref[:, cols] = mixed.astype(o_ref.dtype)

    row_id = lax.broadcasted_iota(jnp.int32, (CHUNK, CHUNK), 0)
    col_id = lax.broadcasted_iota(jnp.int32, (CHUNK, CHUNK), 1)
    for h in range(N_GROUPS):
        cols = slice(h * gc, (h + 1) * gc)
        w_tri = jnp.where(row_id >= col_id, ws_ref[h], 0.0).astype(BF16)
        bias = bs_ref[h]
        for c in range(rows // CHUNK):
            rs = slice(c * CHUNK, (c + 1) * CHUNK)
            vr = v_ref[rs, cols]
            ms = jnp.mean(vr * vr, axis=-1, keepdims=True)
            vn = (vr * lax.rsqrt(ms + EPS) * gv_ref[:, cols]).astype(BF16)
            mixed = _bdot(w_tri, vn) + bias
            o_ref[rs, d_pool + h * gc:d_pool + (h + 1) * gc] = (u_ref[rs, cols] * mixed).astype(o_ref.dtype)


def _mix_prompt(pz, w_pool, pool_scale, g_v, w_s, b_s, *, batch, seq, rows):
    d_pool = pool_scale.shape[-1]
    gc = d_pool // N_GROUPS
    blocks_per_seq = seq // rows

    def row_blk(b, r):
        return b * blocks_per_seq + r

    def halo_blk(b, r):
        return jnp.maximum((b * seq + r * rows) // HALO - 1, 0)

    kern = functools.partial(_mix_prompt_kernel, rows=rows)
    return pl.pallas_call(
        kern,
        grid=(batch, blocks_per_seq),
        in_specs=[
            pl.BlockSpec((rows, d_pool), lambda b, r: (row_blk(b, r), 0)),
            pl.BlockSpec((HALO, d_pool), lambda b, r: (halo_blk(b, r), 0)),
            pl.BlockSpec((rows, d_pool), lambda b, r: (row_blk(b, r), 1)),
            pl.BlockSpec((rows, d_pool), lambda b, r: (row_blk(b, r), 2)),
            pl.BlockSpec((N_GROUPS, gc, gc), lambda b, r: (0, 0, 0)),
            pl.BlockSpec((1, d_pool), lambda b, r: (0, 0)),
            pl.BlockSpec((1, d_pool), lambda b, r: (0, 0)),
            pl.BlockSpec((N_GROUPS, CHUNK, CHUNK), lambda b, r: (0, 0, 0)),
            pl.BlockSpec((N_GROUPS, CHUNK, 1), lambda b, r: (0, 0, 0)),
        ],
        out_specs=pl.BlockSpec((rows, 2 * d_pool), lambda b, r: (row_blk(b, r), 0)),
        out_shape=jax.ShapeDtypeStruct((batch * seq, 2 * d_pool), BF16),
        scratch_shapes=[pltpu.VMEM((HALO + rows, d_pool), F32), pltpu.VMEM((rows, gc), BF16)],
        compiler_params=_params("arbitrary", "arbitrary"),
        name="mix_prompt",
    )(pz, pz, pz, pz, w_pool, pool_scale.reshape(1, d_pool), g_v.reshape(1, d_pool), w_s,
      b_s.reshape(N_GROUPS, CHUNK, 1))


def _mix_sample_kernel(ws_ref, bs_ref, p_ref, st_ref, u_ref, v_ref, wp_ref, ps_ref, gv_ref,
                       mix_ref, ov_ref, ost_ref, pooled_ref, a3_ref, b3_ref, o3_ref, *, n_t):
    s = pl.program_id(0)
    nb, _, gc = a3_ref.shape
    n_state = st_ref.shape[0]

    def split_tokens(dst3_ref, src_ref):
        dst3_ref[...] = src_ref[...].reshape(nb, n_t, gc)

    def ext(e):
        return a3_ref[:, e - n_state, :] if e >= n_state else st_ref[e]

    for g, w in enumerate(POOL_WINDOWS):
        @pl.when(s == g)
        def _(w=w):
            split_tokens(a3_ref, p_ref)
            for t in range(n_t):
                tok = a3_ref[:, t, :]
                acc = tok
                for back in range(1, w):
                    acc = acc + ext(n_state + t - back)
                cnt = float(min(PAST_LEN + t + 1, w))
                pooled_ref[t * nb:(t + 1) * nb, :] = (acc / cnt - tok).astype(BF16)
            mixed = _bdot(pooled_ref[...], wp_ref[0].astype(BF16)) * ps_ref[...]
            for t in range(n_t):
                o3_ref[:, t, :] = mixed[t * nb:(t + 1) * nb]
            mix_ref[...] = o3_ref[...].reshape(nb * n_t, gc)
            for e in range(n_state):
                ost_ref[e] = ext(e + n_t)

    @pl.when(s >= N_GROUPS)
    def _():
        h = s - N_GROUPS
        split_tokens(a3_ref, v_ref)
        split_tokens(b3_ref, u_ref)
        vn = []
        for t in range(n_t):
            vr = a3_ref[:, t, :]
            ms = jnp.mean(vr * vr, axis=-1, keepdims=True)
            vt = vr * lax.rsqrt(ms + EPS) * gv_ref[...]
            ov_ref[:, t, :] = vt
            vn.append(vt)
        for t in range(n_t):
            m = ws_ref[h, t * n_t] * vn[0]
            for j in range(1, t + 1):
                m = m + ws_ref[h, t * n_t + j] * vn[j]
            o3_ref[:, t, :] = b3_ref[:, t, :] * (m + bs_ref[h, t])
        mix_ref[...] = o3_ref[...].reshape(nb * n_t, gc)


def _mix_sample(pz, state_tm, w_pool, pool_scale, g_v, w_s, b_s, *, n_t):
    rows = pz.shape[0]
    nb = rows // n_t
    n_state = state_tm.shape[0]
    d_pool = pool_scale.shape[-1]
    gc = d_pool // N_GROUPS
    ng = N_GROUPS
    ws_small = w_s[:, :n_t, :n_t].reshape(ng, n_t * n_t)
    bs_small = b_s[:, :n_t]
    smem = pl.BlockSpec(memory_space=pltpu.SMEM)
    pool_blk = lambda s: jnp.minimum(s, ng - 1)
    gate_blk = lambda s: jnp.maximum(s - ng, 0)
    return pl.pallas_call(
        functools.partial(_mix_sample_kernel, n_t=n_t),
        grid=(2 * ng,),
        in_specs=[
            smem, smem,
            pl.BlockSpec((rows, gc), lambda s: (0, pool_blk(s))),
            pl.BlockSpec((n_state, nb, gc), lambda s: (0, 0, pool_blk(s))),
            pl.BlockSpec((rows, gc), lambda s: (0, ng + gate_blk(s))),
            pl.BlockSpec((rows, gc), lambda s: (0, 2 * ng + gate_blk(s))),
            pl.BlockSpec((1, gc, gc), lambda s: (pool_blk(s), 0, 0)),
            pl.BlockSpec((1, gc), lambda s: (0, pool_blk(s))),
            pl.BlockSpec((1, gc), lambda s: (0, gate_blk(s))),
        ],
        out_specs=[
            pl.BlockSpec((rows, gc), lambda s: (0, s)),
            pl.BlockSpec((nb, n_t, gc), lambda s: (0, 0, gate_blk(s))),
            pl.BlockSpec((n_state, nb, gc), lambda s: (0, 0, pool_blk(s))),
        ],
        out_shape=[jax.ShapeDtypeStruct((rows, 2 * d_pool), F32),
                   jax.ShapeDtypeStruct((nb, n_t, d_pool), F32),
                   jax.ShapeDtypeStruct((n_state, nb, d_pool), F32)],
        scratch_shapes=[pltpu.VMEM((rows, gc), BF16)] + [pltpu.VMEM((nb, n_t, gc), F32)] * 3,
        compiler_params=_params("arbitrary"),
        name="mix_sample",
    )(ws_small, bs_small, pz, state_tm, pz, pz, w_pool, pool_scale.reshape(1, d_pool),
      g_v.reshape(1, d_pool))


def _softmax_rows(s):
    m = jnp.max(s, axis=-1, keepdims=True)
    e = jnp.exp(s - m)
    return e / jnp.sum(e, axis=-1, keepdims=True)


def _scores_t(q, k):
    return lax.dot_general(q, k, (((1,), (1,)), ((), ())), preferred_element_type=F32)


def _attn_kernel(qs_ref, ks_ref, vs_ref, qp_ref, kp_ref, vp_ref, os_ref, op_ref, *, scale, n_q):
    _, nb, n_mem, n_heads, hd = ks_ref.shape
    rows = n_q * n_heads
    r_head = lax.broadcasted_iota(jnp.int32, (rows, n_mem * n_heads), 0) // n_q
    c_head = lax.broadcasted_iota(jnp.int32, (rows, n_mem * n_heads), 1) % n_heads
    same_head = r_head == c_head

    scores = []
    for b in range(nb):
        qb = jnp.concatenate([qs_ref[b * n_q:(b + 1) * n_q, h * hd:(h + 1) * hd] for h in range(n_heads)], axis=0)
        km = ks_ref[0, b].reshape(n_mem * n_heads, hd).astype(BF16)
        scores.append(_scores_t(qb.astype(BF16), km) * scale)
    scores_p = _scores_t(qp_ref[...], kp_ref[...]) * scale

    probs = [_softmax_rows(jnp.where(same_head, s, MASKED_SCORE)).astype(BF16) for s in scores]
    probs_p = _softmax_rows(scores_p).astype(BF16)

    for b in range(nb):
        vm = vs_ref[0, b].reshape(n_mem * n_heads, hd).astype(BF16)
        ob = _bdot(probs[b], vm)
        for h in range(n_heads):
            os_ref[b * n_q:(b + 1) * n_q, h * hd:(h + 1) * hd] = ob[h * n_q:(h + 1) * n_q, :]
    op_ref[...] = _bdot(probs_p, vp_ref[...]).astype(op_ref.dtype)


def _attention(q_s, mem_k_s, mem_v_s, q_p, k_p, v_p, *, layer, n_q, nb, batch, seq):
    _, dec_batch, n_mem, n_heads, hd = mem_k_s.shape
    rows = n_q
    n_steps = dec_batch // nb
    tq = batch * n_heads * seq // n_steps
    assert tq * n_steps == batch * n_heads * seq and seq % tq == 0 and tq % 16 == 0
    per_pair = seq // tq

    def p_rows(i):
        pair = i // per_pair
        return (pair // n_heads) * per_pair + i % per_pair, pair % n_heads

    def p_mem(i):
        pair = i // per_pair
        return pair // n_heads, pair % n_heads

    mem_spec = pl.BlockSpec((1, nb, n_mem, n_heads, hd), lambda i: (layer, i, 0, 0, 0))
    qs_spec = pl.BlockSpec((nb * rows, n_heads * hd), lambda i: (i, 0))
    return pl.pallas_call(
        functools.partial(_attn_kernel, scale=hd ** -0.5, n_q=n_q),
        grid=(n_steps,),
        in_specs=[
            qs_spec, mem_spec, mem_spec,
            pl.BlockSpec((tq, hd), p_rows), pl.BlockSpec((n_mem, hd), p_mem), pl.BlockSpec((n_mem, hd), p_mem),
        ],
        out_specs=[qs_spec, pl.BlockSpec((tq, hd), p_rows)],
        out_shape=[jax.ShapeDtypeStruct((dec_batch * rows, n_heads * hd), F32),
                   jax.ShapeDtypeStruct((batch * seq, n_heads * hd), BF16)],
        compiler_params=_params("arbitrary"),
        name="attention",
    )(q_s, mem_k_s, mem_v_s, q_p, k_p, v_p)


def _layer_front(x, mixers, lw, *, tm, q_dtype, tag):
    (g_mix, w_in, w_out, g_xattn, w_q, w_o, g_ffn, w_gate_up, w_down, d_pool) = lw
    pz = _norm_mm(x, g_mix, w_in, tm=tm, tn=512, out_dtypes=(F32,), gelu_from_col=d_pool,
                  name="in_proj_" + tag)[0]
    mix, extras = mixers(pz)
    x, xg, rs = _mm_res(mix, w_out, x, tm=tm, tn=512, a_buffers=2, g_next=g_xattn, name="out_proj_" + tag)
    q = _scaled_mm(xg, rs, w_q, tm=tm, tn=512, out_dtypes=(q_dtype,), name="q_proj_" + tag)[0]
    return x, q, pz, extras


def _layer_back(x, o, lw, *, tm, tm_up, tag):
    (g_mix, w_in, w_out, g_xattn, w_q, w_o, g_ffn, w_gate_up, w_down, d_pool) = lw
    x, xg, rs = _mm_res(o, w_o, x, tm=tm, tn=512, a_buffers=2, g_next=g_ffn, name="o_proj_" + tag)
    act = _scaled_swiglu(xg, rs, w_gate_up, tm=tm_up, tn=256, name="ffn_up_" + tag)
    return _mm_res(act, w_down, x, tm=tm, tn=256, a_buffers=1, name="ffn_down_" + tag)


def kernel(x_prompt, x_sample, cache_mem_k, cache_mem_v, state_pool, mem_prompt, g_mix, w_in, g_v, w_pool,
           pool_scale, w_s, b_s, w_out, g_xattn, g_mem, w_q, w_k, w_v, w_o, g_ffn, w_gate_up, w_down, g_final):
    depth = g_mix.shape[0]
    assert depth == 1, "single-layer step"
    batch, seq, d = x_prompt.shape
    dec_batch, dec_seq, _ = x_sample.shape
    n_mem = mem_prompt.shape[1]
    n_heads, hd = cache_mem_k.shape[-2:]
    d_pool = pool_scale.shape[-1]
    n_state = state_pool.shape[2]
    n_sample = dec_batch * dec_seq
    l = 0
    lw = (g_mix[l], w_in[l], w_out[l], g_xattn[l], w_q[l], w_o[l], g_ffn[l], w_gate_up[l], w_down[l], d_pool)

    mem = mem_prompt.reshape(batch * n_mem, d)
    mk, mk_b = _norm_mm(mem, g_mem[l], w_k[l], tm=batch * n_mem, tn=512, out_dtypes=(F32, BF16), name="mem_k")
    mv, mv_b = _norm_mm(mem, g_mem[l], w_v[l], tm=batch * n_mem, tn=512, out_dtypes=(F32, BF16), name="mem_v")

    def prompt_mixers(pz):
        return _mix_prompt(pz, w_pool[l], pool_scale[l], g_v[l], w_s[l], b_s[l],
                           batch=batch, seq=seq, rows=512), None

    def sample_mixers(pz):
        mix, v_s, pool_tm = _mix_sample(pz, state_pool[l].transpose(1, 0, 2), w_pool[l], pool_scale[l], g_v[l],
                                        w_s[l], b_s[l], n_t=dec_seq)
        return mix, (v_s, pool_tm.transpose(1, 0, 2)[None])

    xp, q_p, pz_p, _ = _layer_front(x_prompt.reshape(batch * seq, d), prompt_mixers, lw,
                                    tm=1024, q_dtype=BF16, tag="prompt")
    xs, q_s, _, (v_s, pool_s) = _layer_front(x_sample.reshape(n_sample, d), sample_mixers, lw,
                                             tm=n_sample, q_dtype=F32, tag="sample")

    o_s, o_p = _attention(q_s, cache_mem_k, cache_mem_v, q_p, mk_b, mv_b,
                          layer=l, n_q=dec_seq, nb=2, batch=batch, seq=seq)

    xp = _layer_back(xp, o_p, lw, tm=1024, tm_up=2048, tag="prompt")
    xs = _layer_back(xs, o_s, lw, tm=n_sample, tm_up=n_sample, tag="sample")

    y_prompt = _rmsnorm(xp, g_final, tm=512, name="final_norm_prompt").reshape(batch, seq, d)
    y_sample = _rmsnorm(xs, g_final, tm=n_sample, seq_len=dec_seq, name="final_norm_sample")
    pool_p = pz_p.reshape(batch, seq, -1)[:, seq - n_state:, :d_pool]

    return (y_prompt, y_sample,
            mk.reshape(1, batch, n_mem, n_heads, hd), mv.reshape(1, batch, n_mem, n_heads, hd),
            pool_p[None], pool_s, v_s[None])
```
